```python
import jax, jax.numpy as jnp
from jax import lax
import numpy as np

D_MODEL = 1024
BATCH = 2
SEQ = 16384
DEPTH = 4
DEC_BATCH = 32
DEC_SEQ = 2048
PAST_LEN = 128

N_MIXERS = 3
D_FF = 2816
EPS = 1e-6
N_MOD = 9
A_HEADS = 8
A_DK = 64
A_DV = 128
A_CHUNK = 128
A_GATE_CAP = 15.0
A_IN = 2 * A_HEADS * A_DK + 2 * A_HEADS * A_DV + 4 * A_HEADS
ATT_HEADS = 16
ATT_KV_HEADS = 4
ATT_GROUP = ATT_HEADS // ATT_KV_HEADS
HEAD_DIM = 64
ATT_IN = (ATT_HEADS + 2 * ATT_KV_HEADS) * HEAD_DIM
WINDOW = 128
BLOCK = 128
ROPE_THETA = 10000.0
GRID_W = 64

N_A = len(range(0, DEPTH, N_MIXERS))
N_B = len(range(1, DEPTH, N_MIXERS))
N_C = len(range(2, DEPTH, N_MIXERS))

kernel_name = "hybrid_bidir_mlstm_swa_axial_encoder"


def rmsnorm(x, w):
    xf = x.astype(jnp.float32)
    y = xf * lax.rsqrt(jnp.mean(xf * xf, axis=-1, keepdims=True) + EPS)
    return (y * w.astype(jnp.float32)).astype(x.dtype)


def modulate(h, shift, scale):
    return h * (1 + scale) + shift


def swiglu(h, w13, w2):
    gate, up = jnp.split(h @ w13, 2, axis=-1)
    return (jax.nn.silu(gate) * up) @ w2


def rope_tables(pos, dim):
    inv = ROPE_THETA ** (-jnp.arange(0, dim, 2, dtype=jnp.float32) / dim)
    ang = pos.astype(jnp.float32)[:, None] * inv[None, :]
    ang = jnp.concatenate([ang, ang], axis=-1)
    return jnp.cos(ang), jnp.sin(ang)


def apply_rope(x, cos, sin):
    x1, x2 = jnp.split(x, 2, axis=-1)
    rot = jnp.concatenate([-x2, x1], axis=-1)
    return x * cos[:, None].astype(x.dtype) + rot * sin[:, None].astype(x.dtype)


def mlstm_scan(q, k, v, ig, lf):
    B, H, S, DK = q.shape
    DV = v.shape[-1]
    n_ch = S // A_CHUNK

    def to_chunks(a):
        a = a.reshape((B, H, n_ch, A_CHUNK) + a.shape[3:])
        return jnp.moveaxis(a, 2, 0)

    xs = tuple(to_chunks(a) for a in (q, k, v, ig, lf))
    lower = jnp.tril(jnp.ones((A_CHUNK, A_CHUNK), dtype=bool))

    def step(carry, inp):
        C, n, m = carry
        qi, ki, vi, ii, fi = inp
        b = jnp.cumsum(fi, axis=-1)
        log_d = b[..., :, None] - b[..., None, :] + ii[..., None, :]
        log_d = jnp.where(lower, log_d, -jnp.inf)
        inter = b + m[..., None]
        m_t = jnp.maximum(inter, jnp.max(log_d, axis=-1))
        dmat = jnp.exp(log_d - m_t[..., None])
        w_inter = jnp.exp(inter - m_t)
        s = jnp.einsum('bhtd,bhsd->bhts', qi, ki) * dmat
        num = jnp.einsum('bhts,bhsv->bhtv', s, vi) + w_inter[..., None] * jnp.einsum('bhtd,bhdv->bhtv', qi, C)
        den = jnp.sum(s, axis=-1) + w_inter * jnp.einsum('bhtd,bhd->bht', qi, n)
        h = num / jnp.maximum(jnp.abs(den), jnp.exp(-m_t))[..., None]
        b_last = b[..., -1]
        log_w = b_last[..., None] - b + ii
        m_new = jnp.maximum(b_last + m, jnp.max(log_w, axis=-1))
        w = jnp.exp(log_w - m_new[..., None])
        decay = jnp.exp(b_last + m - m_new)
        C = decay[..., None, None] * C + jnp.einsum('bhs,bhsd,bhsv->bhdv', w, ki, vi)
        n = decay[..., None] * n + jnp.einsum('bhs,bhsd->bhd', w, ki)
        return (C, n, m_new), h

    init = (jnp.zeros((B, H, DK, DV), jnp.float32), jnp.zeros((B, H, DK), jnp.float32),
            jnp.full((B, H), -jnp.inf, jnp.float32))
    _, hc = lax.scan(step, init, xs)
    return jnp.moveaxis(hc, 0, 2).reshape(B, H, S, DV)


def mlstm_mixer(h, w_in, b_gate, norm_w, w_out):
    B, S, _ = h.shape
    qk = A_HEADS * A_DK
    vd = A_HEADS * A_DV
    proj = h @ w_in
    q, k, v, o, g = jnp.split(proj, [qk, 2 * qk, 2 * qk + vd, 2 * qk + 2 * vd], axis=-1)

    def heads(a, d):
        return a.reshape(B, S, A_HEADS, d).transpose(0, 2, 1, 3).astype(jnp.float32)

    q = heads(q, A_DK) * (A_DK ** -0.5)
    k = heads(k, A_DK)
    v = heads(v, A_DV)
    g = g.astype(jnp.float32) + b_gate.astype(jnp.float32)
    g = A_GATE_CAP * jnp.tanh(g / A_GATE_CAP)
    g = g.reshape(B, S, 4, A_HEADS).transpose(2, 0, 3, 1)
    ig_f, lf_f = g[0], jax.nn.log_sigmoid(g[1])
    ig_b, lf_b = g[2], jax.nn.log_sigmoid(g[3])
    h_f = mlstm_scan(q, k, v, ig_f, lf_f)
    flip = lambda a: jnp.flip(a, axis=2)
    h_b = flip(mlstm_scan(flip(q), flip(k), flip(v), flip(ig_b), flip(lf_b)))
    hs = (h_f + h_b).transpose(0, 2, 1, 3)
    hs = rmsnorm(hs, norm_w.reshape(A_HEADS, A_DV)).reshape(B, S, vd).astype(h.dtype)
    return (hs * jax.nn.sigmoid(o)) @ w_out


def attn_qkv(h, w_in, q_norm, k_norm):
    B, S, _ = h.shape
    proj = h @ w_in
    qd = ATT_HEADS * HEAD_DIM
    kd = ATT_KV_HEADS * HEAD_DIM
    q = proj[..., :qd].reshape(B, S, ATT_HEADS, HEAD_DIM)
    k = proj[..., qd:qd + kd].reshape(B, S, ATT_KV_HEADS, HEAD_DIM)
    v = proj[..., qd + kd:].reshape(B, S, ATT_KV_HEADS, HEAD_DIM)
    return rmsnorm(q, q_norm), rmsnorm(k, k_norm), v


def swa_mixer(h, w_in, q_norm, k_norm, sink, w_out, cos, sin):
    B, S, _ = h.shape
    nb = S // BLOCK
    q, k, v = attn_qkv(h, w_in, q_norm, k_norm)
    q = apply_rope(q, cos, sin)
    k = apply_rope(k, cos, sin)
    qb = jnp.moveaxis(q.reshape(B, nb, BLOCK, ATT_KV_HEADS, ATT_GROUP, HEAD_DIM), 1, 0)

    def windows(a):
        ap = jnp.pad(a, ((0, 0), (BLOCK, BLOCK), (0, 0), (0, 0))).reshape(B, nb + 2, BLOCK, ATT_KV_HEADS, HEAD_DIM)
        aw = jnp.concatenate([ap[:, :-2], ap[:, 1:-1], ap[:, 2:]], axis=2)
        return jnp.moveaxis(aw, 1, 0)

    kw, vw = windows(k), windows(v)
    qi = jnp.arange(BLOCK)[:, None]
    kj = jnp.arange(3 * BLOCK)[None, :] - BLOCK
    band = jnp.abs(qi - kj) <= WINDOW
    kpos = jnp.arange(nb)[:, None] * BLOCK - BLOCK + jnp.arange(3 * BLOCK)[None, :]
    kvalid = (kpos >= 0) & (kpos < S)
    sink_l = sink.astype(jnp.float32).reshape(ATT_KV_HEADS, ATT_GROUP)[None, :, :, None, None]
    scale = HEAD_DIM ** -0.5

    def block(args):
        qblk, kblk, vblk, valid = args
        s = jnp.einsum('bqkgd,bskd->bkgqs', qblk, kblk).astype(jnp.float32) * scale
        s = jnp.where(band & valid[None, :], s, -jnp.inf)
        sk = jnp.broadcast_to(sink_l, s.shape[:-1] + (1,))
        p = jax.nn.softmax(jnp.concatenate([s, sk], axis=-1), axis=-1)[..., :-1]
        return jnp.einsum('bkgqs,bskd->bqkgd', p.astype(vblk.dtype), vblk)

    o = lax.map(block, (qb, kw, vw, kvalid))
    o = jnp.moveaxis(o, 0, 1).reshape(B, S, ATT_HEADS * HEAD_DIM)
    return o @ w_out


def axial_mixer(h, w_in, q_norm, k_norm, w_out, row_cs, col_cs):
    B, S, _ = h.shape
    nb = S // BLOCK
    half = HEAD_DIM // 2
    q, k, v = attn_qkv(h, w_in, q_norm, k_norm)

    def axial(a):
        return jnp.concatenate([apply_rope(a[..., :half], *row_cs), apply_rope(a[..., half:], *col_cs)], axis=-1)

    q, k = axial(q), axial(k)
    qb = jnp.moveaxis(q.reshape(B, nb, BLOCK, ATT_KV_HEADS, ATT_GROUP, HEAD_DIM), 1, 0)
    scale = HEAD_DIM ** -0.5

    def block(qblk):
        s = jnp.einsum('bqkgd,bskd->bkgqs', qblk, k).astype(jnp.float32) * scale
        p = jax.nn.softmax(s, axis=-1)
        return jnp.einsum('bkgqs,bskd->bqkgd', p.astype(v.dtype), v)

    o = lax.map(block, qb)
    o = jnp.moveaxis(o, 0, 1).reshape(B, S, ATT_HEADS * HEAD_DIM)
    return o @ w_out


def trunk(x, c, ffn_w13, ffn_w2, ada_w, ada_b, norm_w,
          mlstm_w_in, mlstm_b_gate, mlstm_norm_w, mlstm_w_out,
          swa_w_in, swa_q_norm, swa_k_norm, swa_sink, swa_w_out,
          axial_w_in, axial_q_norm, axial_k_norm, axial_w_out):
    B, S, _ = x.shape
    rope_cos, rope_sin = rope_tables(jnp.arange(S), HEAD_DIM)
    rows = S // GRID_W
    row_ids = jnp.repeat(jnp.arange(rows), GRID_W)
    col_ids = jnp.tile(jnp.arange(GRID_W), rows)
    row_cs = rope_tables(row_ids, HEAD_DIM // 2)
    col_cs = rope_tables(col_ids, HEAD_DIM // 2)
    c_act = jax.nn.silu(c)
    for i in range(DEPTH):
        mod = (c_act @ ada_w[i] + ada_b[i])[:, None, :]
        sh1, sc1, g1, sh2, sc2, g2, sh3, sc3, g3 = jnp.split(mod, N_MOD, axis=-1)
        h = modulate(rmsnorm(x, norm_w[i, 0]), sh1, sc1)
        x = x + 0.5 * g1 * swiglu(h, ffn_w13[i, 0], ffn_w2[i, 0])
        h = modulate(rmsnorm(x, norm_w[i, 1]), sh2, sc2)
        kind, j = i % N_MIXERS, i // N_MIXERS
        if kind == 0:
            mix = mlstm_mixer(h, mlstm_w_in[j], mlstm_b_gate[j], mlstm_norm_w[j], mlstm_w_out[j])
        elif kind == 1:
            mix = swa_mixer(h, swa_w_in[j], swa_q_norm[j], swa_k_norm[j], swa_sink[j], swa_w_out[j], rope_cos, rope_sin)
        else:
            mix = axial_mixer(h, axial_w_in[j], axial_q_norm[j], axial_k_norm[j], axial_w_out[j], row_cs, col_cs)
        x = x + g2 * mix
        h = modulate(rmsnorm(x, norm_w[i, 2]), sh3, sc3)
        x = x + 0.5 * g3 * swiglu(h, ffn_w13[i, 1], ffn_w2[i, 1])
    return x


def setup_inputs(seed: int = 0) -> dict:
    key = jax.random.key(seed)
    ks = jax.random.split(key, 24)
    f32 = jnp.float32
    nrm = lambda k, shape, s: jax.random.normal(k, shape, f32) * s
    gate_base = jnp.repeat(jnp.array([0.0, 3.0, 0.0, 3.0], f32), A_HEADS)
    return {
        "x_prompt": nrm(ks[0], (BATCH, SEQ, D_MODEL), 1.0),
        "x_sample": nrm(ks[1], (DEC_BATCH, DEC_SEQ, D_MODEL), 1.0),
        "c_prompt": nrm(ks[2], (BATCH, D_MODEL), 1.0),
        "c_sample": nrm(ks[3], (DEC_BATCH, D_MODEL), 1.0),
        "ffn_w13": nrm(ks[4], (DEPTH, 2, D_MODEL, 2 * D_FF), D_MODEL ** -0.5),
        "ffn_w2": nrm(ks[5], (DEPTH, 2, D_FF, D_MODEL), D_FF ** -0.5),
        "ada_w": nrm(ks[6], (DEPTH, D_MODEL, N_MOD * D_MODEL), 0.5 * D_MODEL ** -0.5),
        "ada_b": nrm(ks[7], (DEPTH, N_MOD * D_MODEL), 0.02),
        "norm_w": 1.0 + nrm(ks[8], (DEPTH, 3, D_MODEL), 0.02),
        "mlstm_w_in": nrm(ks[9], (N_A, D_MODEL, A_IN), D_MODEL ** -0.5),
        "mlstm_b_gate": gate_base[None, :] + nrm(ks[10], (N_A, 4 * A_HEADS), 0.1),
        "mlstm_norm_w": 1.0 + nrm(ks[11], (N_A, A_HEADS * A_DV), 0.02),
        "mlstm_w_out": nrm(ks[12], (N_A, A_HEADS * A_DV, D_MODEL), (A_HEADS * A_DV) ** -0.5),
        "swa_w_in": nrm(ks[13], (N_B, D_MODEL, ATT_IN), D_MODEL ** -0.5),
        "swa_q_norm": 1.0 + nrm(ks[14], (N_B, HEAD_DIM), 0.02),
        "swa_k_norm": 1.0 + nrm(ks[15], (N_B, HEAD_DIM), 0.02),
        "swa_sink": nrm(ks[16], (N_B, ATT_HEADS), 0.5),
        "swa_w_out": nrm(ks[17], (N_B, ATT_HEADS * HEAD_DIM, D_MODEL), (ATT_HEADS * HEAD_DIM) ** -0.5),
        "axial_w_in": nrm(ks[18], (N_C, D_MODEL, ATT_IN), D_MODEL ** -0.5),
        "axial_q_norm": 1.0 + nrm(ks[19], (N_C, HEAD_DIM), 0.02),
        "axial_k_norm": 1.0 + nrm(ks[20], (N_C, HEAD_DIM), 0.02),
        "axial_w_out": nrm(ks[21], (N_C, ATT_HEADS * HEAD_DIM, D_MODEL), (ATT_HEADS * HEAD_DIM) ** -0.5),
    }


def reference(x_prompt, x_sample, c_prompt, c_sample, ffn_w13, ffn_w2, ada_w, ada_b, norm_w,
              mlstm_w_in, mlstm_b_gate, mlstm_norm_w, mlstm_w_out,
              swa_w_in, swa_q_norm, swa_k_norm, swa_sink, swa_w_out,
              axial_w_in, axial_q_norm, axial_k_norm, axial_w_out):
    params = (ffn_w13, ffn_w2, ada_w, ada_b, norm_w,
              mlstm_w_in, mlstm_b_gate, mlstm_norm_w, mlstm_w_out,
              swa_w_in, swa_q_norm, swa_k_norm, swa_sink, swa_w_out,
              axial_w_in, axial_q_norm, axial_k_norm, axial_w_out)
    y_prompt = trunk(x_prompt, c_prompt, *params)
    y_sample = trunk(x_sample, c_sample, *params)
    return (y_prompt, y_sample)
```

```python
import functools

import jax
import jax.numpy as jnp
from jax import lax
from jax.experimental import pallas as pl
from jax.experimental.pallas import tpu as pltpu

F32 = jnp.float32
BF16 = jnp.bfloat16

D_MODEL = 1024
DEPTH = 4
N_MIXERS = 3
D_FF = 2816
EPS = 1e-6
N_MOD = 9
A_HEADS = 8
A_DK = 64
A_DV = 128
A_CHUNK = 128
A_GATE_CAP = 15.0
ATT_HEADS = 16
ATT_KV_HEADS = 4
ATT_GROUP = ATT_HEADS // ATT_KV_HEADS
HEAD_DIM = 64
WINDOW = 128
BLOCK = 128
ROPE_THETA = 10000.0
GRID_W = 64

V7X_VMEM_BYTES = 64 * 1024 * 1024
V7X_LANES = 128
V7X_MXU_DIM = 256

FFN_CHUNK = V7X_MXU_DIM
N_FFN_CHUNKS = D_FF // FFN_CHUNK
QK_A = A_HEADS * A_DK
V_A = A_HEADS * A_DV
Q_ATT = ATT_HEADS * HEAD_DIM
KV_ATT = ATT_KV_HEADS * HEAD_DIM

NT_DIMS = (((1,), (1,)), ((), ()))


def _vmem_limit(est_bytes):
    return int(min(est_bytes * 5 // 4 + (4 << 20), V7X_VMEM_BYTES - (4 << 20)))


def _params(sem, est_bytes):
    return pltpu.CompilerParams(dimension_semantics=sem, vmem_limit_bytes=_vmem_limit(est_bytes))


def _resident(shape):
    nd = len(shape)
    return pl.BlockSpec(shape, lambda *_: (0,) * nd, pipeline_mode=pl.Buffered(1))


def _dot(a, b):
    return jnp.dot(a, b, preferred_element_type=F32)


def _dot_nt(a, b):
    return lax.dot_general(a, b, NT_DIMS, preferred_element_type=F32)


def _norm_mod(x, nw, shift, scale):
    ms = jnp.mean(x * x, axis=-1, keepdims=True)
    y = x * lax.rsqrt(ms + EPS)
    return (y * nw) * (1.0 + scale) + shift


def _sigmoid(x):
    return 1.0 / (1.0 + jnp.exp(-x))


def _ada_kernel(c_ref, w_ref, b_ref, o_ref):
    c = c_ref[...]
    a = c * _sigmoid(c)
    o_ref[...] = _dot(a, w_ref[...]) + b_ref[...]


def _ada_mod(c_all, ada_w, ada_b):
    bc = c_all.shape[0]
    n_out = N_MOD * D_MODEL
    tn = D_MODEL
    return pl.pallas_call(
        _ada_kernel,
        grid=(DEPTH, n_out // tn),
        in_specs=[
            pl.BlockSpec((bc, D_MODEL), lambda i, n: (0, 0)),
            pl.BlockSpec((None, D_MODEL, tn), lambda i, n: (i, 0, n)),
            pl.BlockSpec((None, 1, tn), lambda i, n: (i, 0, n)),
        ],
        out_specs=pl.BlockSpec((None, bc, tn), lambda i, n: (i, 0, n)),
        out_shape=jax.ShapeDtypeStruct((DEPTH, bc, n_out), F32),
        compiler_params=_params(("parallel", "parallel"), 4 * (2 * D_MODEL * tn + 4 * bc * tn)),
        name="ada_mod",
    )(c_all, ada_w, ada_b.reshape(DEPTH, 1, n_out))


def _ffn_kernel(x_ref, mod_ref, nw_ref, w1_ref, w3_ref, w2_ref, o_ref, h_ref, acc_ref, *, row0):
    x = x_ref[...]
    shift = mod_ref[row0:row0 + 1, :]
    scale = mod_ref[row0 + 1:row0 + 2, :]
    gate = mod_ref[row0 + 2:row0 + 3, :]
    h_ref[...] = _norm_mod(x, nw_ref[...], shift, scale).astype(BF16)
    acc_ref[...] = jnp.zeros_like(acc_ref)

    def body(c, carry):
        h = h_ref[...]
        g = _dot(h, w1_ref[c])
        u = _dot(h, w3_ref[c])
        a = (g * _sigmoid(g)) * u
        acc_ref[...] += _dot(a.astype(BF16), w2_ref[c])
        return carry

    lax.fori_loop(0, N_FFN_CHUNKS, body, 0)
    o_ref[...] = x + (0.5 * gate) * acc_ref[...]


def _ffn(x, mod, nw, w1, w3, w2, *, row0, tm):
    b, s, _ = x.shape
    est = 4 * tm * D_MODEL * 4 + 3 * D_MODEL * D_FF * 2 + tm * D_MODEL * 6 + 4 * tm * FFN_CHUNK * 4
    return pl.pallas_call(
        functools.partial(_ffn_kernel, row0=row0),
        grid=(b, s // tm),
        in_specs=[
            pl.BlockSpec((None, tm, D_MODEL), lambda i, m: (i, m, 0)),
            pl.BlockSpec((None, N_MOD, D_MODEL), lambda i, m: (i, 0, 0)),
            pl.BlockSpec((1, D_MODEL), lambda i, m: (0, 0)),
            _resident(w1.shape),
            _resident(w3.shape),
            _resident(w2.shape),
        ],
        out_specs=pl.BlockSpec((None, tm, D_MODEL), lambda i, m: (i, m, 0)),
        out_shape=jax.ShapeDtypeStruct(x.shape, F32),
        scratch_shapes=[pltpu.VMEM((tm, D_MODEL), BF16), pltpu.VMEM((tm, D_MODEL), F32)],
        compiler_params=_params(("parallel", "parallel"), est),
        name="ffn",
    )(x, mod, nw, w1, w3, w2)


def _log_sigmoid(x):
    return jnp.minimum(x, 0.0) - jnp.log1p(jnp.exp(-jnp.abs(x)))


def _gate_act(g, is_forget):
    g = A_GATE_CAP * jnp.tanh(g / A_GATE_CAP)
    return jnp.where(is_forget, _log_sigmoid(g), g)


def _mlstm_in_kernel(x_ref, mod_ref, nw_ref, wq_ref, wkt_ref, wv_ref, wg_ref, wgt_ref, bg_ref, bgt_ref,
                     q_ref, kt_ref, v_ref, grow_ref, gcol_ref):
    h = _norm_mod(x_ref[...], nw_ref[...], mod_ref[3:4, :], mod_ref[4:5, :]).astype(BF16)
    q_ref[...] = (_dot(h, wq_ref[...]) * (A_DK ** -0.5)).astype(BF16)
    v_ref[...] = _dot(h, wv_ref[...]).astype(BF16)
    kt = _dot_nt(wkt_ref[...], h).astype(BF16)
    g_col = _dot(h, wg_ref[...]) + bg_ref[...]
    g_row = _dot_nt(wgt_ref[...], h) + bgt_ref[...]
    col_id = lax.broadcasted_iota(jnp.int32, g_col.shape, 1)
    row_id = lax.broadcasted_iota(jnp.int32, g_row.shape, 0)
    gcol_ref[...] = _gate_act(g_col, (col_id // A_HEADS) % 2 == 1)
    g_row = _gate_act(g_row, (row_id // A_HEADS) % 2 == 1)
    for j in range(kt_ref.shape[0]):
        kt_ref[j] = kt[:, j * A_CHUNK:(j + 1) * A_CHUNK]
        grow_ref[j] = g_row[:, j * A_CHUNK:(j + 1) * A_CHUNK]


def _mlstm_in(x, mod, nw, wq, wkt, wv, wg, wgt, bg, bgt, *, tm):
    b, s, _ = x.shape
    nch, cpt = s // A_CHUNK, tm // A_CHUNK
    ng = 4 * A_HEADS
    est = 2 * tm * D_MODEL * 4 + 2 * (D_MODEL * (2 * QK_A + V_A + 2 * V7X_LANES)) * 2 \
        + 2 * tm * (QK_A * 2 * 2 + V_A * 2 + 2 * V7X_LANES * 4) + tm * (2 * QK_A + V_A) * 4
    return pl.pallas_call(
        _mlstm_in_kernel,
        grid=(b, s // tm),
        in_specs=[
            pl.BlockSpec((None, tm, D_MODEL), lambda i, m: (i, m, 0)),
            pl.BlockSpec((None, N_MOD, D_MODEL), lambda i, m: (i, 0, 0)),
            pl.BlockSpec((1, D_MODEL), lambda i, m: (0, 0)),
            _resident(wq.shape), _resident(wkt.shape), _resident(wv.shape),
            _resident(wg.shape), _resident(wgt.shape), _resident(bg.shape), _resident(bgt.shape),
        ],
        out_specs=[
            pl.BlockSpec((None, tm, QK_A), lambda i, m: (i, m, 0)),
            pl.BlockSpec((None, cpt, QK_A, A_CHUNK), lambda i, m: (i, m, 0, 0)),
            pl.BlockSpec((None, tm, V_A), lambda i, m: (i, m, 0)),
            pl.BlockSpec((None, cpt, ng, A_CHUNK), lambda i, m: (i, m, 0, 0)),
            pl.BlockSpec((None, tm, ng), lambda i, m: (i, m, 0)),
        ],
        out_shape=[
            jax.ShapeDtypeStruct((b, s, QK_A), BF16),
            jax.ShapeDtypeStruct((b, nch, QK_A, A_CHUNK), BF16),
            jax.ShapeDtypeStruct((b, s, V_A), BF16),
            jax.ShapeDtypeStruct((b, nch, ng, A_CHUNK), F32),
            jax.ShapeDtypeStruct((b, s, ng), F32),
        ],
        compiler_params=_params(("parallel", "parallel"), est),
        name="mlstm_in",
    )(x, mod, nw, wq, wkt, wv, wg, wgt, bg, bgt)


def _split3(a):
    hi = a.astype(BF16)
    r1 = a - hi.astype(F32)
    mid = r1.astype(BF16)
    lo = (r1 - mid.astype(F32)).astype(BF16)
    return hi, mid, lo


def _cumsum_rows(a, tri):
    hi, mid, lo = _split3(a)
    return _dot(hi, tri) + _dot(mid, tri) + _dot(lo, tri)


def _cumsum_cols(tri, a):
    hi, mid, lo = _split3(a)
    return _dot(tri, hi) + _dot(tri, mid) + _dot(tri, lo)


def _mlstm_scan_kernel(qf_ref, ktf_ref, vf_ref, growf_ref, gcolf_ref,
                       qb_ref, ktb_ref, vb_ref, growb_ref, gcolb_ref,
                       hf_ref, hb_ref, c_ref, m_ref, *, cps):
    L = A_CHUNK
    H = A_HEADS

    @pl.when(pl.program_id(1) == 0)
    def _():
        c_ref[...] = jnp.zeros_like(c_ref)
        m_ref[...] = jnp.full(m_ref.shape, -jnp.inf, F32)

    ri = lax.broadcasted_iota(jnp.int32, (L, L), 0)
    ci = lax.broadcasted_iota(jnp.int32, (L, L), 1)
    le = ri <= ci
    ge = ri >= ci
    tri_le = jnp.where(le, 1.0, 0.0).astype(BF16)
    tri_ge = jnp.where(ge, 1.0, 0.0).astype(BF16)
    ones_col = jnp.where(ci == 0, 1.0, 0.0).astype(BF16)

    def one_direction(d, c, q_ref, kt_ref, v_ref, grow_ref, gcol_ref, out_ref):
        r0 = pl.multiple_of(c * L, L)
        grow = grow_ref[c]
        gcol = gcol_ref[pl.ds(r0, L), :]
        ig_rows = grow[2 * d * H:(2 * d + 1) * H, :]
        lf_rows = grow[(2 * d + 1) * H:(2 * d + 2) * H, :]
        lf_cols = gcol[:, (2 * d + 1) * H:(2 * d + 2) * H]
        if d == 0:
            b_rows = _cumsum_rows(lf_rows, tri_le)
            b_cols = _cumsum_cols(tri_ge, lf_cols)
            keep = ge
        else:
            b_rows = _cumsum_rows(lf_rows, tri_ge)
            b_cols = _cumsum_cols(tri_le, lf_cols)
            keep = le
        q_all = q_ref[pl.ds(r0, L), :]
        kt_all = kt_ref[c]
        for hd in range(H):
            sd = d * H + hd
            q = q_all[:, hd * A_DK:(hd + 1) * A_DK]
            kt = kt_all[hd * A_DK:(hd + 1) * A_DK, :]
            v = v_ref[pl.ds(r0, L), hd * A_DV:(hd + 1) * A_DV]
            v_ext = jnp.concatenate([v, ones_col], axis=1)
            b_row = b_rows[hd:hd + 1, :]
            i_row = ig_rows[hd:hd + 1, :]
            b_col = b_cols[:, hd:hd + 1]
            m_prev = m_ref[sd:sd + 1, :]
            c_prev = c_ref[sd]

            log_d = jnp.where(keep, (b_col - b_row) + i_row, -jnp.inf)
            inter = b_col + m_prev[:, 0:1]
            m_t = jnp.maximum(inter, jnp.max(log_d, axis=1, keepdims=True))
            dmat = jnp.exp(log_d - m_t)
            w_inter = jnp.exp(inter - m_t)
            s_mat = (_dot(q, kt) * dmat).astype(BF16)
            r = _dot(s_mat, v_ext) + w_inter * _dot(q, c_prev.astype(BF16))
            num = r[:, :L]
            den = r[:, L:L + 1]
            out_ref[pl.ds(r0, L), hd * A_DV:(hd + 1) * A_DV] = num / jnp.maximum(jnp.abs(den), jnp.exp(-m_t))

            b_last = b_row[:, L - 1:L] if d == 0 else b_row[:, 0:1]
            log_w = (b_last - b_row) + i_row
            m_new = jnp.maximum(b_last + m_prev, jnp.max(log_w, axis=1, keepdims=True))
            w_row = jnp.exp(log_w - m_new)
            decay = jnp.exp((b_last + m_prev) - m_new)
            kw = (kt.astype(F32) * w_row).astype(BF16)
            c_ref[sd] = decay[:, 0:1] * c_prev + _dot(kw, v_ext)
            m_ref[sd:sd + 1, :] = m_new

    def body(c, carry):
        one_direction(0, c, qf_ref, ktf_ref, vf_ref, growf_ref, gcolf_ref, hf_ref)
        one_direction(1, cps - 1 - c, qb_ref, ktb_ref, vb_ref, growb_ref, gcolb_ref, hb_ref)
        return carry

    lax.fori_loop(0, cps, body, 0)


def _mlstm_scan(q, kt, v, grow, gcol, *, tb):
    b, s, _ = q.shape
    cps = tb // A_CHUNK
    nb = s // tb
    ng = 4 * A_HEADS

    def specs(idx):
        return [
            pl.BlockSpec((None, tb, QK_A), lambda i, j: (i, idx(j), 0)),
            pl.BlockSpec((None, cps, QK_A, A_CHUNK), lambda i, j: (i, idx(j), 0, 0)),
            pl.BlockSpec((None, tb, V_A), lambda i, j: (i, idx(j), 0)),
            pl.BlockSpec((None, cps, ng, A_CHUNK), lambda i, j: (i, idx(j), 0, 0)),
            pl.BlockSpec((None, tb, ng), lambda i, j: (i, idx(j), 0)),
        ]

    fwd = lambda j: j
    bwd = lambda j: nb - 1 - j
    est = 2 * 2 * tb * (QK_A * 2 * 2 + V_A * 2 + 2 * V7X_LANES * 4) + 2 * 2 * tb * V_A * 4 \
        + 2 * A_HEADS * A_DK * 2 * A_CHUNK * 4 + (8 << 20)
    return pl.pallas_call(
        functools.partial(_mlstm_scan_kernel, cps=cps),
        grid=(b, nb),
        in_specs=specs(fwd) + specs(bwd),
        out_specs=[
            pl.BlockSpec((None, tb, V_A), lambda i, j: (i, j, 0)),
            pl.BlockSpec((None, tb, V_A), lambda i, j: (i, nb - 1 - j, 0)),
        ],
        out_shape=[jax.ShapeDtypeStruct((b, s, V_A), F32)] * 2,
        scratch_shapes=[
            pltpu.VMEM((2 * A_HEADS, A_DK, 2 * A_CHUNK), F32),
            pltpu.VMEM((2 * A_HEADS, A_CHUNK), F32),
        ],
        compiler_params=_params(("parallel", "arbitrary"), est),
        name="mlstm_scan",
    )(q, kt, v, grow, gcol, q, kt, v, grow, gcol)


def _mlstm_out_kernel(x_ref, hf_ref, hb_ref, mod_ref, nw_ref, wo_ref, mnw_ref, wout_ref, o_ref):
    x = x_ref[...]
    h = _norm_mod(x, nw_ref[...], mod_ref[3:4, :], mod_ref[4:5, :]).astype(BF16)
    o_gate = _dot(h, wo_ref[...])
    hs = hf_ref[...] + hb_ref[...]
    parts = []
    for hd in range(A_HEADS):
        a = hs[:, hd * A_DV:(hd + 1) * A_DV]
        ms = jnp.mean(a * a, axis=-1, keepdims=True)
        parts.append(a * lax.rsqrt(ms + EPS))
    y = jnp.concatenate(parts, axis=1) * mnw_ref[...]
    z = (y * _sigmoid(o_gate)).astype(BF16)
    o_ref[...] = x + mod_ref[5:6, :] * _dot(z, wout_ref[...])


def _mlstm_out(x, hf, hb, mod, nw, wo, mnw, wout, *, tm):
    b, s, _ = x.shape
    tile = pl.BlockSpec((None, tm, D_MODEL), lambda i, m: (i, m, 0))
    est = 2 * 4 * tm * D_MODEL * 4 + 2 * D_MODEL * D_MODEL * 2 + 6 * tm * D_MODEL * 4
    return pl.pallas_call(
        _mlstm_out_kernel,
        grid=(b, s // tm),
        in_specs=[
            tile, tile, tile,
            pl.BlockSpec((None, N_MOD, D_MODEL), lambda i, m: (i, 0, 0)),
            pl.BlockSpec((1, D_MODEL), lambda i, m: (0, 0)),
            _resident(wo.shape),
            pl.BlockSpec((1, V_A), lambda i, m: (0, 0)),
            _resident(wout.shape),
        ],
        out_specs=tile,
        out_shape=jax.ShapeDtypeStruct(x.shape, F32),
        compiler_params=_params(("parallel", "parallel"), est),
        name="mlstm_out",
    )(x, hf, hb, mod, nw, wo, mnw, wout)


def _attn_in_kernel(x_ref, mod_ref, nw_ref, wq_ref, wqr_ref, wkt_ref, wkrt_ref, wv_ref, seg_ref,
                    cq_ref, sq_ref, ckt_ref, skt_ref, qw_ref, qwsw_ref, kw_ref, kwsw_ref,
                    q_ref, kt_ref, v_ref, *, q_scale):
    h = _norm_mod(x_ref[...], nw_ref[...], mod_ref[3:4, :], mod_ref[4:5, :]).astype(BF16)
    v_ref[...] = _dot(h, wv_ref[...]).astype(BF16)

    q = _dot(h, wq_ref[...])
    q_rot = _dot(h, wqr_ref[...])
    q2 = q * q
    q2_hi = q2.astype(BF16)
    q2_lo = (q2 - q2_hi.astype(F32)).astype(BF16)
    seg = seg_ref[...]
    ssq = _dot(q2_hi, seg) + _dot(q2_lo, seg)
    rq = lax.rsqrt(ssq * (1.0 / HEAD_DIM) + EPS)
    reps = Q_ATT // V7X_LANES
    cos_q = jnp.concatenate([cq_ref[...] * qw_ref[...]] * reps, axis=1)
    sin_q = jnp.concatenate([sq_ref[...] * qwsw_ref[...]] * reps, axis=1)
    q_ref[...] = ((rq * q_scale) * (q * cos_q + q_rot * sin_q)).astype(BF16)

    kt = _dot_nt(wkt_ref[...], h)
    kt_rot = _dot_nt(wkrt_ref[...], h)
    cos_k = ckt_ref[...] * kw_ref[...]
    sin_k = skt_ref[...] * kwsw_ref[...]
    outs = []
    for j in range(ATT_KV_HEADS):
        a = kt[j * HEAD_DIM:(j + 1) * HEAD_DIM, :]
        ar = kt_rot[j * HEAD_DIM:(j + 1) * HEAD_DIM, :]
        rk = lax.rsqrt(jnp.mean(a * a, axis=0, keepdims=True) + EPS)
        outs.append(rk * (a * cos_k + ar * sin_k))
    kt_ref[...] = jnp.concatenate(outs, axis=0).astype(BF16)


def _attn_in(x, mod, nw, w, tabs, *, tm, tk, q_scale):
    b, s, _ = x.shape
    per = tk // tm
    const2 = lambda i, m: (0, 0)
    est = 2 * tm * D_MODEL * 4 + (3 * D_MODEL * Q_ATT + 3 * D_MODEL * KV_ATT) * 2 \
        + 2 * tm * (Q_ATT + 2 * KV_ATT) * 2 + 8 * tm * Q_ATT * 4 + 8 * tm * V7X_LANES * 4
    return pl.pallas_call(
        functools.partial(_attn_in_kernel, q_scale=q_scale),
        grid=(b, s // tm),
        in_specs=[
            pl.BlockSpec((None, tm, D_MODEL), lambda i, m: (i, m, 0)),
            pl.BlockSpec((None, N_MOD, D_MODEL), lambda i, m: (i, 0, 0)),
            pl.BlockSpec((1, D_MODEL), const2),
            _resident(w["wq"].shape), _resident(w["wqr"].shape),
            _resident(w["wkt"].shape), _resident(w["wkrt"].shape),
            _resident(w["wv"].shape), _resident(w["seg"].shape),
            pl.BlockSpec((tm, V7X_LANES), lambda i, m: (m, 0)),
            pl.BlockSpec((tm, V7X_LANES), lambda i, m: (m, 0)),
            pl.BlockSpec((HEAD_DIM, tm), lambda i, m: (0, m)),
            pl.BlockSpec((HEAD_DIM, tm), lambda i, m: (0, m)),
            pl.BlockSpec((1, V7X_LANES), const2), pl.BlockSpec((1, V7X_LANES), const2),
            pl.BlockSpec((HEAD_DIM, 1), const2), pl.BlockSpec((HEAD_DIM, 1), const2),
        ],
        out_specs=[
            pl.BlockSpec((None, tm, Q_ATT), lambda i, m: (i, m, 0)),
            pl.BlockSpec((None, None, KV_ATT, tm), lambda i, m: (i, m // per, 0, m % per)),
            pl.BlockSpec((None, tm, KV_ATT), lambda i, m: (i, m, 0)),
        ],
        out_shape=[
            jax.ShapeDtypeStruct((b, s, Q_ATT), BF16),
            jax.ShapeDtypeStruct((b, s // tk, KV_ATT, tk), BF16),
            jax.ShapeDtypeStruct((b, s, KV_ATT), BF16),
        ],
        compiler_params=_params(("parallel", "parallel"), est),
        name="attn_in",
    )(x, mod, nw, w["wq"], w["wqr"], w["wkt"], w["wkrt"], w["wv"], w["seg"],
      tabs["cos_q"], tabs["sin_q"], tabs["cos_kt"], tabs["sin_kt"],
      w["qw"], w["qwsw"], w["kw"], w["kwsw"])


def _attn_out_kernel(x_ref, o_ref_in, mod_ref, wout_ref, o_ref):
    o_ref[...] = x_ref[...] + mod_ref[5:6, :] * _dot(o_ref_in[...], wout_ref[...])


def _attn_out(x, o, mod, wout, *, tm):
    b, s, _ = x.shape
    tile = pl.BlockSpec((None, tm, D_MODEL), lambda i, m: (i, m, 0))
    est = 2 * tm * D_MODEL * (4 + 2 + 4) + D_MODEL * D_MODEL * 2 + 2 * tm * D_MODEL * 4
    return pl.pallas_call(
        _attn_out_kernel,
        grid=(b, s // tm),
        in_specs=[tile, tile, pl.BlockSpec((None, N_MOD, D_MODEL), lambda i, m: (i, 0, 0)), _resident(wout.shape)],
        out_specs=tile,
        out_shape=jax.ShapeDtypeStruct(x.shape, F32),
        compiler_params=_params(("parallel", "parallel"), est),
        name="attn_out",
    )(x, o, mod, wout)


def _stack_heads(q_rows, g):
    base = g * ATT_GROUP * HEAD_DIM
    return jnp.concatenate(
        [q_rows[:, base + a * HEAD_DIM:base + (a + 1) * HEAD_DIM] for a in range(ATT_GROUP)], axis=0)


def _unstack_heads(o, rows):
    return jnp.concatenate([o[a * rows:(a + 1) * rows, :] for a in range(ATT_GROUP)], axis=1)


def _swa_kernel(sink_ref, q_ref, ktl_ref, ktm_ref, ktr_ref, vl_ref, vm_ref, vr_ref, o_ref, *, seq, tq):
    nsub = tq // BLOCK
    j = pl.program_id(1)
    kt_win = jnp.concatenate([ktl_ref[...], ktm_ref[...], ktr_ref[...]], axis=1)
    v_win = jnp.concatenate([vl_ref[...], vm_ref[...], vr_ref[...]], axis=0)
    rows = ATT_GROUP * BLOCK
    qi = lax.broadcasted_iota(jnp.int32, (rows, 3 * BLOCK), 0) % BLOCK
    kj = lax.broadcasted_iota(jnp.int32, (rows, 3 * BLOCK), 1) - BLOCK
    band = jnp.abs(qi - kj) <= WINDOW
    for i in range(nsub):
        kpos = kj + (j * tq + i * BLOCK)
        mask = band & (kpos >= 0) & (kpos < seq)
        q_rows = q_ref[i * BLOCK:(i + 1) * BLOCK, :]
        outs = []
        for g in range(ATT_KV_HEADS):
            kt = kt_win[g * HEAD_DIM:(g + 1) * HEAD_DIM, i * BLOCK:(i + 3) * BLOCK]
            v = v_win[i * BLOCK:(i + 3) * BLOCK, g * HEAD_DIM:(g + 1) * HEAD_DIM]
            s = jnp.where(mask, _dot(_stack_heads(q_rows, g), kt), -jnp.inf)
            sink = jnp.concatenate(
                [jnp.full((BLOCK, 1), sink_ref[g * ATT_GROUP + a], F32) for a in range(ATT_GROUP)], axis=0)
            m = jnp.maximum(jnp.max(s, axis=1, keepdims=True), sink)
            p = jnp.exp(s - m)
            denom = jnp.sum(p, axis=1, keepdims=True) + jnp.exp(sink - m)
            o = _dot(p.astype(BF16), v) / denom
            outs.append(_unstack_heads(o, BLOCK))
        o_ref[i * BLOCK:(i + 1) * BLOCK, :] = jnp.concatenate(outs, axis=1).astype(BF16)


def _swa(q, kt, v, sink, *, tq):
    b, s, _ = q.shape
    nsub = tq // BLOCK
    nblk = s // BLOCK
    nq = s // tq
    left = lambda j: jnp.maximum(j * nsub - 1, 0)
    right = lambda j: jnp.minimum((j + 1) * nsub, nblk - 1)
    est = 2 * (tq * Q_ATT * 2 * 2 + 2 * (tq + 2 * BLOCK) * KV_ATT * 2) + (12 << 20)
    return pl.pallas_call(
        functools.partial(_swa_kernel, seq=s, tq=tq),
        grid=(b, nq),
        in_specs=[
            pl.BlockSpec(memory_space=pltpu.SMEM),
            pl.BlockSpec((None, tq, Q_ATT), lambda i, j: (i, j, 0)),
            pl.BlockSpec((None, None, KV_ATT, BLOCK), lambda i, j: (i, 0, 0, left(j))),
            pl.BlockSpec((None, None, KV_ATT, tq), lambda i, j: (i, 0, 0, j)),
            pl.BlockSpec((None, None, KV_ATT, BLOCK), lambda i, j: (i, 0, 0, right(j))),
            pl.BlockSpec((None, BLOCK, KV_ATT), lambda i, j: (i, left(j), 0)),
            pl.BlockSpec((None, tq, KV_ATT), lambda i, j: (i, j, 0)),
            pl.BlockSpec((None, BLOCK, KV_ATT), lambda i, j: (i, right(j), 0)),
        ],
        out_specs=pl.BlockSpec((None, tq, Q_ATT), lambda i, j: (i, j, 0)),
        out_shape=jax.ShapeDtypeStruct((b, s, Q_ATT), BF16),
        compiler_params=_params(("parallel", "parallel"), est),
        name="swa",
    )(sink, q, kt, kt, kt, v, v, v)


def _axial_kernel(q_ref, kt_ref, v_ref, o_ref, m_ref, l_ref, acc_ref, *, tk, nk):
    tq = q_ref.shape[0]
    rows = ATT_GROUP * tq
    q_rows = q_ref[...]
    outs = []
    for g in range(ATT_KV_HEADS):
        q = _stack_heads(q_rows, g)
        m_ref[...] = jnp.full(m_ref.shape, -jnp.inf, F32)
        l_ref[...] = jnp.zeros_like(l_ref)
        acc_ref[...] = jnp.zeros_like(acc_ref)

        def body(c, carry):
            kt = kt_ref[c][g * HEAD_DIM:(g + 1) * HEAD_DIM, :]
            v = v_ref[pl.ds(pl.multiple_of(c * tk, tk), tk), :][:, g * HEAD_DIM:(g + 1) * HEAD_DIM]
            s = _dot(q, kt)
            m_prev = m_ref[...]
            m_new = jnp.maximum(m_prev, jnp.max(s, axis=1, keepdims=True))
            alpha = jnp.exp(m_prev - m_new)
            p = jnp.exp(s - m_new)
            l_ref[...] = alpha * l_ref[...] + jnp.sum(p, axis=1, keepdims=True)
            acc_ref[...] = alpha * acc_ref[...] + _dot(p.astype(BF16), v)
            m_ref[...] = m_new
            return carry

        lax.fori_loop(0, nk, body, 0)
        outs.append(_unstack_heads(acc_ref[...] / l_ref[...], tq))
    o_ref[...] = jnp.concatenate(outs, axis=1).astype(BF16)


def _axial(q, kt, v, *, tq, tk):
    b, s, _ = q.shape
    nk = s // tk
    rows = ATT_GROUP * tq
    est = 2 * tq * Q_ATT * 2 * 2 + 2 * s * KV_ATT * 2 + 3 * rows * V7X_LANES * 4 + rows * tk * 12 + (4 << 20)
    return pl.pallas_call(
        functools.partial(_axial_kernel, tk=tk, nk=nk),
        grid=(b, s // tq),
        in_specs=[
            pl.BlockSpec((None, tq, Q_ATT), lambda i, j: (i, j, 0)),
            pl.BlockSpec((None, nk, KV_ATT, tk), lambda i, j: (i, 0, 0, 0), pipeline_mode=pl.Buffered(1)),
            pl.BlockSpec((None, s, KV_ATT), lambda i, j: (i, 0, 0), pipeline_mode=pl.Buffered(1)),
        ],
        out_specs=pl.BlockSpec((None, tq, Q_ATT), lambda i, j: (i, j, 0)),
        out_shape=jax.ShapeDtypeStruct((b, s, Q_ATT), BF16),
        scratch_shapes=[
            pltpu.VMEM((rows, 1), F32), pltpu.VMEM((rows, 1), F32), pltpu.VMEM((rows, HEAD_DIM), F32),
        ],
        compiler_params=_params(("parallel", "parallel"), est),
        name="axial",
    )(q, kt, v)


def _rope_tables(pos, dim):
    inv = ROPE_THETA ** (-jnp.arange(0, dim, 2, dtype=F32) / dim)
    ang = pos.astype(F32)[:, None] * inv[None, :]
    ang = jnp.concatenate([ang, ang], axis=-1)
    return jnp.cos(ang), jnp.sin(ang)


def _rot_half_perm(widths):
    perm, sign, base = [], [], 0
    for w in widths:
        half = w // 2
        perm += [base + half + i for i in range(half)] + [base + i for i in range(half)]
        sign += [-1.0] * half + [1.0] * half
        base += w
    return jnp.array(perm, jnp.int32), jnp.array(sign, F32)


def _attn_tables(cos, sin):
    reps = V7X_LANES // HEAD_DIM
    return {
        "cos_q": jnp.tile(cos, (1, reps)), "sin_q": jnp.tile(sin, (1, reps)),
        "cos_kt": cos.T, "sin_kt": sin.T,
    }


def _attn_weights(w_in, q_norm, k_norm, widths):
    perm, sign = _rot_half_perm(widths)
    wq = w_in[:, :Q_ATT]
    wk = w_in[:, Q_ATT:Q_ATT + KV_ATT]
    wv = w_in[:, Q_ATT + KV_ATT:]

    def rotated(w, heads):
        w3 = w.reshape(D_MODEL, heads, HEAD_DIM)
        return (w3[:, :, perm] * sign).reshape(D_MODEL, heads * HEAD_DIM)

    seg_id = jnp.arange(Q_ATT) // HEAD_DIM
    reps = V7X_LANES // HEAD_DIM
    return {
        "wq": wq.astype(BF16), "wqr": rotated(wq, ATT_HEADS).astype(BF16),
        "wkt": wk.T.astype(BF16), "wkrt": rotated(wk, ATT_KV_HEADS).T.astype(BF16),
        "wv": wv.astype(BF16),
        "seg": (seg_id[:, None] == seg_id[None, :]).astype(BF16),
        "qw": jnp.tile(q_norm, reps)[None, :], "qwsw": jnp.tile(q_norm[perm], reps)[None, :],
        "kw": k_norm[:, None], "kwsw": k_norm[perm][:, None],
    }


def _ffn_weights(w13, w2):
    def chunked(w):
        return w.reshape(D_MODEL, N_FFN_CHUNKS, FFN_CHUNK).transpose(1, 0, 2).astype(BF16)
    return chunked(w13[:, :D_FF]), chunked(w13[:, D_FF:]), w2.reshape(N_FFN_CHUNKS, FFN_CHUNK, D_MODEL).astype(BF16)


def _mlstm_weights(w_in, b_gate, norm_w, w_out):
    wq = w_in[:, :QK_A]
    wk = w_in[:, QK_A:2 * QK_A]
    wv = w_in[:, 2 * QK_A:2 * QK_A + V_A]
    wo = w_in[:, 2 * QK_A + V_A:2 * QK_A + 2 * V_A]
    wg = w_in[:, 2 * QK_A + 2 * V_A:]
    return {
        "wq": wq.astype(BF16), "wkt": wk.T.astype(BF16), "wv": wv.astype(BF16), "wo": wo.astype(BF16),
        "wg": wg.astype(BF16), "wgt": wg.T.astype(BF16),
        "bg": b_gate[None, :], "bgt": b_gate[:, None],
        "mnw": norm_w[None, :], "wout": w_out.astype(BF16),
    }


def _tile(s, want):
    t = min(s, want)
    assert s % t == 0
    return t


def _trunk(x, mod_all, prep):
    b, s, _ = x.shape
    tm_ffn = _tile(s, 1024)
    tm = _tile(s, 512)
    for i in range(DEPTH):
        mod = mod_all[i].reshape(b, N_MOD, D_MODEL)
        lw = prep["layers"][i]
        x = _ffn(x, mod, lw["nw"][0:1], *lw["ffn0"], row0=0, tm=tm_ffn)
        kind = i % N_MIXERS
        mw = lw["mixer"]
        if kind == 0:
            q, kt, v, grow, gcol = _mlstm_in(x, mod, lw["nw"][1:2], mw["wq"], mw["wkt"], mw["wv"], mw["wg"],
                                             mw["wgt"], mw["bg"], mw["bgt"], tm=tm)
            hf, hb = _mlstm_scan(q, kt, v, grow, gcol, tb=_tile(s, 512))
            x = _mlstm_out(x, hf, hb, mod, lw["nw"][1:2], mw["wo"], mw["mnw"], mw["wout"], tm=tm)
        elif kind == 1:
            q, kt, v = _attn_in(x, mod, lw["nw"][1:2], mw, prep["swa_tabs"][s], tm=tm, tk=s,
                                q_scale=HEAD_DIM ** -0.5)
            o = _swa(q, kt, v, mw["sink"], tq=_tile(s, 512))
            x = _attn_out(x, o, mod, mw["wout"], tm=tm)
        else:
            tk = _tile(s, 2048)
            q, kt, v = _attn_in(x, mod, lw["nw"][1:2], mw, prep["axial_tabs"][s], tm=tm, tk=tk,
                                q_scale=HEAD_DIM ** -0.5)
            o = _axial(q, kt, v, tq=_tile(s, 128), tk=tk)
            x = _attn_out(x, o, mod, mw["wout"], tm=tm)
        x = _ffn(x, mod, lw["nw"][2:3], *lw["ffn1"], row0=6, tm=tm_ffn)
    return x


def _prepare(seqs, ffn_w13, ffn_w2, norm_w,
             mlstm_w_in, mlstm_b_gate, mlstm_norm_w, mlstm_w_out,
             swa_w_in, swa_q_norm, swa_k_norm, swa_sink, swa_w_out,
             axial_w_in, axial_q_norm, axial_k_norm, axial_w_out):
    layers = []
    for i in range(DEPTH):
        kind, j = i % N_MIXERS, i // N_MIXERS
        if kind == 0:
            mixer = _mlstm_weights(mlstm_w_in[j], mlstm_b_gate[j], mlstm_norm_w[j], mlstm_w_out[j])
        elif kind == 1:
            mixer = _attn_weights(swa_w_in[j], swa_q_norm[j], swa_k_norm[j], (HEAD_DIM,))
            mixer["sink"] = swa_sink[j]
            mixer["wout"] = swa_w_out[j].astype(BF16)
        else:
            mixer = _attn_weights(axial_w_in[j], axial_q_norm[j], axial_k_norm[j], (HEAD_DIM // 2, HEAD_DIM // 2))
            mixer["wout"] = axial_w_out[j].astype(BF16)
        layers.append({
            "nw": norm_w[i],
            "ffn0": _ffn_weights(ffn_w13[i, 0], ffn_w2[i, 0]),
            "ffn1": _ffn_weights(ffn_w13[i, 1], ffn_w2[i, 1]),
            "mixer": mixer,
        })
    swa_tabs, axial_tabs = {}, {}
    for s in seqs:
        swa_tabs[s] = _attn_tables(*_rope_tables(jnp.arange(s), HEAD_DIM))
        rows = s // GRID_W
        row_ids = jnp.repeat(jnp.arange(rows), GRID_W)
        col_ids = jnp.tile(jnp.arange(GRID_W), rows)
        rc, rs = _rope_tables(row_ids, HEAD_DIM // 2)
        cc, cs = _rope_tables(col_ids, HEAD_DIM // 2)
        axial_tabs[s] = _attn_tables(jnp.concatenate([rc, cc], axis=-1), jnp.concatenate([rs, cs], axis=-1))
    return {"layers": layers, "swa_tabs": swa_tabs, "axial_tabs": axial_tabs}


def kernel(x_prompt, x_sample, c_prompt, c_sample, ffn_w13, ffn_w2, ada_w, ada_b, norm_w, mlstm_w_in, mlstm_b_gate, mlstm_norm_w, mlstm_w_out, swa_w_in, swa_q_norm, swa_k_norm, swa_sink, swa_w_out, axial_w_in, axial_q_norm, axial_k_norm, axial_w_out):
    prep = _prepare({x_prompt.shape[1], x_sample.shape[1]}, ffn_w13, ffn_w2, norm_w,
                    mlstm_w_in, mlstm_b_gate, mlstm_norm_w, mlstm_w_out,
                    swa_w_in, swa_q_norm, swa_k_norm, swa_sink, swa_w_out,
                    axial_w_in, axial_q_norm, axial_k_norm, axial_w_out)
    nb_prompt = c_prompt.shape[0]
    mod_all = _ada_mod(jnp.concatenate([c_prompt, c_sample], axis=0), ada_w, ada_b)
    y_prompt = _trunk(x_prompt, mod_all[:, :nb_prompt], prep)
    y_sample = _trunk(x_sample, mod_all[:, nb_prompt:], prep)
    return (y_prompt, y_sample)
```

```python
import functools

import jax
import jax.numpy as jnp
from jax import lax
from jax.experimental import pallas as pl
from jax.experimental.pallas import tpu as pltpu

F32 = jnp.float32
BF16 = jnp.bfloat16

D_MODEL = 1024
DEPTH = 4
N_MIXERS = 3
D_FF = 2816
EPS = 1e-6
N_MOD = 9
A_HEADS = 8
A_DK = 64
A_DV = 128
A_CHUNK = 128
A_GATE_CAP = 15.0
ATT_HEADS = 16
ATT_KV_HEADS = 4
ATT_GROUP = ATT_HEADS // ATT_KV_HEADS
HEAD_DIM = 64
WINDOW = 128
BLOCK = 128
ROPE_THETA = 10000.0
GRID_W = 64

V7X_VMEM_BYTES = 64 * 1024 * 1024
V7X_LANES = 128
V7X_MXU_DIM = 256

FFN_CHUNK = V7X_MXU_DIM
N_FFN_CHUNKS = D_FF // FFN_CHUNK
QK_A = A_HEADS * A_DK
V_A = A_HEADS * A_DV
Q_ATT = ATT_HEADS * HEAD_DIM
KV_ATT = ATT_KV_HEADS * HEAD_DIM
VX_ATT = ATT_KV_HEADS * V7X_LANES
LOG2E = 1.4426950408889634
AXIAL_AHEAD = 3
SWA_STAGE_HEADS = 8

NT_DIMS = (((1,), (1,)), ((), ()))


def _vmem_limit(est_bytes):
    return int(min(est_bytes * 5 // 4 + (4 << 20), V7X_VMEM_BYTES - (4 << 20)))


def _params(sem, est_bytes):
    return pltpu.CompilerParams(dimension_semantics=sem, vmem_limit_bytes=_vmem_limit(est_bytes))


def _resident(shape):
    nd = len(shape)
    return pl.BlockSpec(shape, lambda *_: (0,) * nd, pipeline_mode=pl.Buffered(1))


def _dot(a, b):
    return jnp.dot(a, b, preferred_element_type=F32)


def _dot_nt(a, b):
    return lax.dot_general(a, b, NT_DIMS, preferred_element_type=F32)


def _norm_mod(x, nw, shift, scale):
    ms = jnp.mean(x * x, axis=-1, keepdims=True)
    y = x * lax.rsqrt(ms + EPS)
    return (y * nw) * (1.0 + scale) + shift


def _sigmoid(x):
    return 1.0 / (1.0 + jnp.exp(-x))


def _ada_kernel(c_ref, w_ref, b_ref, o_ref):
    c = c_ref[...]
    a = c * _sigmoid(c)
    o_ref[...] = _dot(a, w_ref[...]) + b_ref[...]


def _ada_mod(c_all, ada_w, ada_b):
    bc = c_all.shape[0]
    n_out = N_MOD * D_MODEL
    tn = D_MODEL
    return pl.pallas_call(
        _ada_kernel,
        grid=(DEPTH, n_out // tn),
        in_specs=[
            pl.BlockSpec((bc, D_MODEL), lambda i, n: (0, 0)),
            pl.BlockSpec((None, D_MODEL, tn), lambda i, n: (i, 0, n)),
            pl.BlockSpec((None, 1, tn), lambda i, n: (i, 0, n)),
        ],
        out_specs=pl.BlockSpec((None, bc, tn), lambda i, n: (i, 0, n)),
        out_shape=jax.ShapeDtypeStruct((DEPTH, bc, n_out), F32),
        compiler_params=_params(("parallel", "parallel"), 4 * (2 * D_MODEL * tn + 4 * bc * tn)),
        name="ada_mod",
    )(c_all, ada_w, ada_b.reshape(DEPTH, 1, n_out))


def _ffn_kernel(x_ref, mod_ref, nw_ref, w1_ref, w3_ref, w2_ref, o_ref, h_ref, acc_ref, *, row0):
    x = x_ref[...]
    shift = mod_ref[row0:row0 + 1, :]
    scale = mod_ref[row0 + 1:row0 + 2, :]
    gate = mod_ref[row0 + 2:row0 + 3, :]
    h_ref[...] = _norm_mod(x, nw_ref[...], shift, scale).astype(BF16)
    acc_ref[...] = jnp.zeros_like(acc_ref)

    def body(c, carry):
        h = h_ref[...]
        g = _dot(h, w1_ref[c])
        u = _dot(h, w3_ref[c])
        a = (g * _sigmoid(g)) * u
        acc_ref[...] += _dot(a.astype(BF16), w2_ref[c])
        return carry

    lax.fori_loop(0, N_FFN_CHUNKS, body, 0)
    o_ref[...] = x + (0.5 * gate) * acc_ref[...]


def _ffn(x, mod, nw, w1, w3, w2, *, row0, tm):
    b, s, _ = x.shape
    est = 4 * tm * D_MODEL * 4 + 3 * D_MODEL * D_FF * 2 + tm * D_MODEL * 6 + 4 * tm * FFN_CHUNK * 4
    return pl.pallas_call(
        functools.partial(_ffn_kernel, row0=row0),
        grid=(b, s // tm),
        in_specs=[
            pl.BlockSpec((None, tm, D_MODEL), lambda i, m: (i, m, 0)),
            pl.BlockSpec((None, N_MOD, D_MODEL), lambda i, m: (i, 0, 0)),
            pl.BlockSpec((1, D_MODEL), lambda i, m: (0, 0)),
            _resident(w1.shape),
            _resident(w3.shape),
            _resident(w2.shape),
        ],
        out_specs=pl.BlockSpec((None, tm, D_MODEL), lambda i, m: (i, m, 0)),
        out_shape=jax.ShapeDtypeStruct(x.shape, F32),
        scratch_shapes=[pltpu.VMEM((tm, D_MODEL), BF16), pltpu.VMEM((tm, D_MODEL), F32)],
        compiler_params=_params(("parallel", "parallel"), est),
        name="ffn",
    )(x, mod, nw, w1, w3, w2)


def _log_sigmoid(x):
    return jnp.minimum(x, 0.0) - jnp.log1p(jnp.exp(-jnp.abs(x)))


def _gate_act(g, is_forget):
    g = A_GATE_CAP * jnp.tanh(g / A_GATE_CAP)
    return jnp.where(is_forget, _log_sigmoid(g), g)


def _mlstm_in_kernel(x_ref, mod_ref, nw_ref, wq_ref, wkt_ref, wv_ref, wgt_ref, bgt_ref,
                     q_ref, kt_ref, v_ref, grow_ref):
    h = _norm_mod(x_ref[...], nw_ref[...], mod_ref[3:4, :], mod_ref[4:5, :]).astype(BF16)
    q_ref[...] = (_dot(h, wq_ref[...]) * (A_DK ** -0.5)).astype(BF16)
    v_ref[...] = _dot(h, wv_ref[...]).astype(BF16)
    kt = _dot_nt(wkt_ref[...], h).astype(BF16)
    g_row = _dot_nt(wgt_ref[...], h) + bgt_ref[...]
    row_id = lax.broadcasted_iota(jnp.int32, g_row.shape, 0)
    g_row = _gate_act(g_row, (row_id // A_HEADS) % 2 == 1)
    for j in range(kt_ref.shape[0]):
        kt_ref[j] = kt[:, j * A_CHUNK:(j + 1) * A_CHUNK]
        grow_ref[j] = g_row[:, j * A_CHUNK:(j + 1) * A_CHUNK]


def _mlstm_in(x, mod, nw, wq, wkt, wv, wgt, bgt, *, tm):
    b, s, _ = x.shape
    nch, cpt = s // A_CHUNK, tm // A_CHUNK
    ng = 4 * A_HEADS
    est = 2 * tm * D_MODEL * 4 + 2 * (D_MODEL * (2 * QK_A + V_A + V7X_LANES)) * 2 \
        + 2 * tm * (QK_A * 2 * 2 + V_A * 2 + V7X_LANES * 4) + tm * (2 * QK_A + V_A) * 4
    return pl.pallas_call(
        _mlstm_in_kernel,
        grid=(b, s // tm),
        in_specs=[
            pl.BlockSpec((None, tm, D_MODEL), lambda i, m: (i, m, 0)),
            pl.BlockSpec((None, N_MOD, D_MODEL), lambda i, m: (i, 0, 0)),
            pl.BlockSpec((1, D_MODEL), lambda i, m: (0, 0)),
            _resident(wq.shape), _resident(wkt.shape), _resident(wv.shape),
            _resident(wgt.shape), _resident(bgt.shape),
        ],
        out_specs=[
            pl.BlockSpec((None, tm, QK_A), lambda i, m: (i, m, 0)),
            pl.BlockSpec((None, cpt, QK_A, A_CHUNK), lambda i, m: (i, m, 0, 0)),
            pl.BlockSpec((None, tm, V_A), lambda i, m: (i, m, 0)),
            pl.BlockSpec((None, cpt, ng, A_CHUNK), lambda i, m: (i, m, 0, 0)),
        ],
        out_shape=[
            jax.ShapeDtypeStruct((b, s, QK_A), BF16),
            jax.ShapeDtypeStruct((b, nch, QK_A, A_CHUNK), BF16),
            jax.ShapeDtypeStruct((b, s, V_A), BF16),
            jax.ShapeDtypeStruct((b, nch, ng, A_CHUNK), F32),
        ],
        compiler_params=_params(("parallel", "parallel"), est),
        name="mlstm_in",
    )(x, mod, nw, wq, wkt, wv, wgt, bgt)


def _split3(a):
    hi = a.astype(BF16)
    r1 = a - hi.astype(F32)
    mid = r1.astype(BF16)
    lo = (r1 - mid.astype(F32)).astype(BF16)
    return hi, mid, lo


def _cumsum_rows(a, tri):
    hi, mid, lo = _split3(a)
    return _dot(hi, tri) + _dot(mid, tri) + _dot(lo, tri)


def _transpose_rows(eye, a):
    hi, mid, lo = _split3(a)
    return _dot_nt(eye, hi) + _dot_nt(eye, mid) + _dot_nt(eye, lo)


def _cummax_lanes(x, lane, reverse):
    n = x.shape[1]
    k = 1
    while k < n:
        if reverse:
            shifted = jnp.where(lane < n - k, pltpu.roll(x, n - k, axis=1), -jnp.inf)
        else:
            shifted = jnp.where(lane >= k, pltpu.roll(x, k, axis=1), -jnp.inf)
        x = jnp.maximum(x, shifted)
        k *= 2
    return x


def _mlstm_scan_kernel(qf_ref, ktf_ref, vf_ref, growf_ref, qb_ref, ktb_ref, vb_ref, growb_ref,
                       hf_ref, hb_ref, c_ref, m_ref, *, cps):
    L = A_CHUNK
    H = A_HEADS

    @pl.when(pl.program_id(1) == 0)
    def _():
        c_ref[...] = jnp.zeros_like(c_ref)
        m_ref[...] = jnp.full(m_ref.shape, -jnp.inf, F32)

    ri = lax.broadcasted_iota(jnp.int32, (L, L), 0)
    ci = lax.broadcasted_iota(jnp.int32, (L, L), 1)
    keeps = (ri >= ci, ri <= ci)
    tris = (jnp.where(ri <= ci, 1.0, 0.0).astype(BF16), jnp.where(ri >= ci, 1.0, 0.0).astype(BF16))
    eye = jnp.where(ri == ci, 1.0, 0.0).astype(BF16)
    ones_blk = jnp.ones((L, L), BF16)
    lane8 = lax.broadcasted_iota(jnp.int32, (H, L), 1)
    refs = ((qf_ref, ktf_ref, vf_ref, growf_ref, hf_ref), (qb_ref, ktb_ref, vb_ref, growb_ref, hb_ref))

    def body(c, carry):
        stages = []
        for d in range(2):
            q_ref, kt_ref, v_ref, grow_ref, out_ref = refs[d]
            cc = c if d == 0 else cps - 1 - c
            r0 = pl.multiple_of(cc * L, L)
            grow = grow_ref[cc]
            ig = grow[2 * d * H:(2 * d + 1) * H, :]
            lf = grow[(2 * d + 1) * H:(2 * d + 2) * H, :]
            b = _cumsum_rows(lf, tris[d])
            z = b - ig
            m_prev = m_ref[d * H:(d + 1) * H, :]
            u = -jnp.maximum(m_prev, _cummax_lanes(-z, lane8, reverse=(d == 1)))
            ut = _transpose_rows(eye, jnp.concatenate([u, u - b], axis=0))
            b_last = b[:, L - 1:L] if d == 0 else b[:, 0:1]
            log_w = (b_last - b) + ig
            m_new = jnp.maximum(b_last + m_prev, jnp.max(log_w, axis=1, keepdims=True))
            w = jnp.exp(log_w - m_new)
            decay = jnp.exp((b_last + m_prev) - m_new)
            m_ref[d * H:(d + 1) * H, :] = m_new
            q_all = q_ref[pl.ds(r0, L), :]
            kt_all = kt_ref[cc]
            for hd in range(H):
                stages.append(dict(
                    d=d, sd=d * H + hd, out_ref=out_ref, r0=r0, hd=hd,
                    q=q_all[:, hd * A_DK:(hd + 1) * A_DK], kt=kt_all[hd * A_DK:(hd + 1) * A_DK, :],
                    v_ext=jnp.concatenate([v_ref[pl.ds(r0, L), hd * A_DV:(hd + 1) * A_DV], ones_blk], axis=1),
                    z=z[hd:hd + 1, :], m_prev=m_prev[hd:hd + 1, :], w=w[hd:hd + 1, :],
                    decay=decay[hd:hd + 1, 0:1], u_col=ut[:, hd:hd + 1], nm_col=ut[:, H + hd:H + hd + 1]))
        for st in stages:
            st["c_prev"] = c_ref[st["sd"]]
            st["qk"] = _dot(st["q"], st["kt"])
            st["qc"] = _dot(st["q"], st["c_prev"].astype(BF16))
        for st in stages:
            u_b = jnp.broadcast_to(st["u_col"], (L, L))
            st["dmat"] = jnp.exp(jnp.where(keeps[st["d"]], u_b - st["z"], -jnp.inf))
            w_inter = jnp.exp(u_b + st["m_prev"])
            st["w_inter"] = jnp.concatenate([w_inter, w_inter], axis=1)
        for st in stages:
            st["sv"] = _dot((st["qk"] * st["dmat"]).astype(BF16), st["v_ext"])
            kw = (st["kt"].astype(F32) * st["w"]).astype(BF16)
            st["upd"] = _dot(kw, st["v_ext"])
        for st in stages:
            r = st["sv"] + st["w_inter"] * st["qc"]
            floor = jnp.exp(jnp.broadcast_to(st["nm_col"], (L, L)))
            hd = st["hd"]
            st["out_ref"][pl.ds(st["r0"], L), hd * A_DV:(hd + 1) * A_DV] = \
                r[:, :L] / jnp.maximum(jnp.abs(r[:, L:]), floor)
            c_ref[st["sd"]] = st["decay"] * st["c_prev"] + st["upd"]
        return carry

    lax.fori_loop(0, cps, body, 0)


def _mlstm_scan(q, kt, v, grow, *, tb):
    b, s, _ = q.shape
    cps = tb // A_CHUNK
    nb = s // tb
    ng = 4 * A_HEADS

    def specs(idx):
        return [
            pl.BlockSpec((None, tb, QK_A), lambda i, j: (i, idx(j), 0)),
            pl.BlockSpec((None, cps, QK_A, A_CHUNK), lambda i, j: (i, idx(j), 0, 0)),
            pl.BlockSpec((None, tb, V_A), lambda i, j: (i, idx(j), 0)),
            pl.BlockSpec((None, cps, ng, A_CHUNK), lambda i, j: (i, idx(j), 0, 0)),
        ]

    fwd = lambda j: j
    bwd = lambda j: nb - 1 - j
    est = 2 * 2 * tb * (QK_A * 2 * 2 + V_A * 2 + V7X_LANES * 4) + 2 * 2 * tb * V_A * 4 \
        + 2 * A_HEADS * A_DK * 2 * A_CHUNK * 4 + (16 << 20)
    return pl.pallas_call(
        functools.partial(_mlstm_scan_kernel, cps=cps),
        grid=(b, nb),
        in_specs=specs(fwd) + specs(bwd),
        out_specs=[
            pl.BlockSpec((None, tb, V_A), lambda i, j: (i, j, 0)),
            pl.BlockSpec((None, tb, V_A), lambda i, j: (i, nb - 1 - j, 0)),
        ],
        out_shape=[jax.ShapeDtypeStruct((b, s, V_A), F32)] * 2,
        scratch_shapes=[
            pltpu.VMEM((2 * A_HEADS, A_DK, 2 * A_CHUNK), F32),
            pltpu.VMEM((2 * A_HEADS, A_CHUNK), F32),
        ],
        compiler_params=_params(("parallel", "arbitrary"), est),
        name="mlstm_scan",
    )(q, kt, v, grow, q, kt, v, grow)


def _mlstm_out_kernel(x_ref, hf_ref, hb_ref, mod_ref, nw_ref, wo_ref, mnw_ref, wout_ref, o_ref):
    x = x_ref[...]
    h = _norm_mod(x, nw_ref[...], mod_ref[3:4, :], mod_ref[4:5, :]).astype(BF16)
    o_gate = _dot(h, wo_ref[...])
    hs = hf_ref[...] + hb_ref[...]
    parts = []
    for hd in range(A_HEADS):
        a = hs[:, hd * A_DV:(hd + 1) * A_DV]
        ms = jnp.mean(a * a, axis=-1, keepdims=True)
        parts.append(a * lax.rsqrt(ms + EPS))
    y = jnp.concatenate(parts, axis=1) * mnw_ref[...]
    z = (y * _sigmoid(o_gate)).astype(BF16)
    o_ref[...] = x + mod_ref[5:6, :] * _dot(z, wout_ref[...])


def _mlstm_out(x, hf, hb, mod, nw, wo, mnw, wout, *, tm):
    b, s, _ = x.shape
    tile = pl.BlockSpec((None, tm, D_MODEL), lambda i, m: (i, m, 0))
    est = 2 * 4 * tm * D_MODEL * 4 + 2 * D_MODEL * D_MODEL * 2 + 6 * tm * D_MODEL * 4
    return pl.pallas_call(
        _mlstm_out_kernel,
        grid=(b, s // tm),
        in_specs=[
            tile, tile, tile,
            pl.BlockSpec((None, N_MOD, D_MODEL), lambda i, m: (i, 0, 0)),
            pl.BlockSpec((1, D_MODEL), lambda i, m: (0, 0)),
            _resident(wo.shape),
            pl.BlockSpec((1, V_A), lambda i, m: (0, 0)),
            _resident(wout.shape),
        ],
        out_specs=tile,
        out_shape=jax.ShapeDtypeStruct(x.shape, F32),
        compiler_params=_params(("parallel", "parallel"), est),
        name="mlstm_out",
    )(x, hf, hb, mod, nw, wo, mnw, wout)


def _attn_in_kernel(x_ref, mod_ref, nw_ref, wq_ref, wqr_ref, wkt_ref, wkrt_ref, wv_ref, seg_ref,
                    cq_ref, sq_ref, ckt_ref, skt_ref, qw_ref, qwsw_ref, kw_ref, kwsw_ref,
                    q_ref, kt_ref, v_ref, *, q_scale):
    h = _norm_mod(x_ref[...], nw_ref[...], mod_ref[3:4, :], mod_ref[4:5, :]).astype(BF16)
    v = _dot(h, wv_ref[...])
    lane = lax.broadcasted_iota(jnp.int32, v.shape, 1)
    v_ref[...] = jnp.where(lane % V7X_LANES == HEAD_DIM, 1.0, v).astype(BF16)

    q = _dot(h, wq_ref[...])
    q_rot = _dot(h, wqr_ref[...])
    q2 = q * q
    q2_hi = q2.astype(BF16)
    q2_lo = (q2 - q2_hi.astype(F32)).astype(BF16)
    seg = seg_ref[...]
    ssq = _dot(q2_hi, seg) + _dot(q2_lo, seg)
    rq = lax.rsqrt(ssq * (1.0 / HEAD_DIM) + EPS)
    reps = Q_ATT // V7X_LANES
    cos_q = jnp.concatenate([cq_ref[...] * qw_ref[...]] * reps, axis=1)
    sin_q = jnp.concatenate([sq_ref[...] * qwsw_ref[...]] * reps, axis=1)
    q_ref[...] = ((rq * q_scale) * (q * cos_q + q_rot * sin_q)).astype(BF16)

    kt = _dot_nt(wkt_ref[...], h)
    kt_rot = _dot_nt(wkrt_ref[...], h)
    cos_k = ckt_ref[...] * kw_ref[...]
    sin_k = skt_ref[...] * kwsw_ref[...]
    outs = []
    for j in range(ATT_KV_HEADS):
        a = kt[j * HEAD_DIM:(j + 1) * HEAD_DIM, :]
        ar = kt_rot[j * HEAD_DIM:(j + 1) * HEAD_DIM, :]
        rk = lax.rsqrt(jnp.mean(a * a, axis=0, keepdims=True) + EPS)
        outs.append(rk * (a * cos_k + ar * sin_k))
    kt_ref[...] = jnp.concatenate(outs, axis=0).astype(BF16)


def _attn_in(x, mod, nw, w, tabs, *, tm, tk, q_scale):
    b, s, _ = x.shape
    per = tk // tm
    const2 = lambda i, m: (0, 0)
    est = 2 * tm * D_MODEL * 4 + (3 * D_MODEL * Q_ATT + 3 * D_MODEL * KV_ATT) * 2 \
        + 2 * tm * (Q_ATT + 2 * KV_ATT) * 2 + 8 * tm * Q_ATT * 4 + 8 * tm * V7X_LANES * 4
    return pl.pallas_call(
        functools.partial(_attn_in_kernel, q_scale=q_scale),
        grid=(b, s // tm),
        in_specs=[
            pl.BlockSpec((None, tm, D_MODEL), lambda i, m: (i, m, 0)),
            pl.BlockSpec((None, N_MOD, D_MODEL), lambda i, m: (i, 0, 0)),
            pl.BlockSpec((1, D_MODEL), const2),
            _resident(w["wq"].shape), _resident(w["wqr"].shape),
            _resident(w["wkt"].shape), _resident(w["wkrt"].shape),
            _resident(w["wv"].shape), _resident(w["seg"].shape),
            pl.BlockSpec((tm, V7X_LANES), lambda i, m: (m, 0)),
            pl.BlockSpec((tm, V7X_LANES), lambda i, m: (m, 0)),
            pl.BlockSpec((HEAD_DIM, tm), lambda i, m: (0, m)),
            pl.BlockSpec((HEAD_DIM, tm), lambda i, m: (0, m)),
            pl.BlockSpec((1, V7X_LANES), const2), pl.BlockSpec((1, V7X_LANES), const2),
            pl.BlockSpec((HEAD_DIM, 1), const2), pl.BlockSpec((HEAD_DIM, 1), const2),
        ],
        out_specs=[
            pl.BlockSpec((None, tm, Q_ATT), lambda i, m: (i, m, 0)),
            pl.BlockSpec((None, None, KV_ATT, tm), lambda i, m: (i, m // per, 0, m % per)),
            pl.BlockSpec((None, tm, VX_ATT), lambda i, m: (i, m, 0)),
        ],
        out_shape=[
            jax.ShapeDtypeStruct((b, s, Q_ATT), BF16),
            jax.ShapeDtypeStruct((b, s // tk, KV_ATT, tk), BF16),
            jax.ShapeDtypeStruct((b, s, VX_ATT), BF16),
        ],
        compiler_params=_params(("parallel", "parallel"), est),
        name="attn_in",
    )(x, mod, nw, w["wq"], w["wqr"], w["wkt"], w["wkrt"], w["wv"], w["seg"],
      tabs["cos_q"], tabs["sin_q"], tabs["cos_kt"], tabs["sin_kt"],
      w["qw"], w["qwsw"], w["kw"], w["kwsw"])


def _attn_out_kernel(x_ref, o_ref_in, mod_ref, wout_ref, o_ref):
    o_ref[...] = x_ref[...] + mod_ref[5:6, :] * _dot(o_ref_in[...], wout_ref[...])


def _attn_out(x, o, mod, wout, *, tm):
    b, s, _ = x.shape
    tile = pl.BlockSpec((None, tm, D_MODEL), lambda i, m: (i, m, 0))
    est = 2 * tm * D_MODEL * (4 + 2 + 4) + D_MODEL * D_MODEL * 2 + 2 * tm * D_MODEL * 4
    return pl.pallas_call(
        _attn_out_kernel,
        grid=(b, s // tm),
        in_specs=[tile, tile, pl.BlockSpec((None, N_MOD, D_MODEL), lambda i, m: (i, 0, 0)), _resident(wout.shape)],
        out_specs=tile,
        out_shape=jax.ShapeDtypeStruct(x.shape, F32),
        compiler_params=_params(("parallel", "parallel"), est),
        name="attn_out",
    )(x, o, mod, wout)


def _stack_heads(q_rows, g):
    base = g * ATT_GROUP * HEAD_DIM
    return jnp.concatenate(
        [q_rows[:, base + a * HEAD_DIM:base + (a + 1) * HEAD_DIM] for a in range(ATT_GROUP)], axis=0)


def _unstack_heads(o, rows):
    return jnp.concatenate([o[a * rows:(a + 1) * rows, :] for a in range(ATT_GROUP)], axis=1)


def _swa_kernel(sink_ref, q_ref, ktl_ref, ktm_ref, ktr_ref, vl_ref, vm_ref, vr_ref, o_ref, *, seq, tq):
    nsub = tq // BLOCK
    j = pl.program_id(1)
    kt_win = jnp.concatenate([ktl_ref[...], ktm_ref[...], ktr_ref[...]], axis=1)
    v_win = jnp.concatenate([vl_ref[...], vm_ref[...], vr_ref[...]], axis=0)
    qi = lax.broadcasted_iota(jnp.int32, (BLOCK, 3 * BLOCK), 0)
    kj = lax.broadcasted_iota(jnp.int32, (BLOCK, 3 * BLOCK), 1) - BLOCK
    band_bias = jnp.where(jnp.abs(qi - kj) <= WINDOW, 0.0, -jnp.inf)
    biases = []
    for i in range(nsub):
        bias = band_bias
        if i == 0:
            bias = jnp.where(kj + j * tq >= 0, bias, -jnp.inf)
        if i == nsub - 1:
            bias = jnp.where(kj + (j * tq + i * BLOCK) < seq, bias, -jnp.inf)
        biases.append(bias)
    q_tiles = [q_ref[i * BLOCK:(i + 1) * BLOCK, :] for i in range(nsub)]

    per_blk = ATT_HEADS // SWA_STAGE_HEADS
    n_stage = nsub * per_blk

    def heads_of(n):
        i, part = divmod(n, per_blk)
        return i, range(part * SWA_STAGE_HEADS, (part + 1) * SWA_STAGE_HEADS)

    def scores(n):
        i, heads = heads_of(n)
        return jnp.concatenate(
            [_dot(q_tiles[i][:, h * HEAD_DIM:(h + 1) * HEAD_DIM],
                  kt_win[(h // ATT_GROUP) * HEAD_DIM:(h // ATT_GROUP + 1) * HEAD_DIM, i * BLOCK:(i + 3) * BLOCK])
             + biases[i] for h in heads], axis=0)

    s_next = scores(0)
    outs = []
    for n in range(n_stage):
        s = s_next
        if n + 1 < n_stage:
            s_next = scores(n + 1)
        i, heads = heads_of(n)
        sink = jnp.concatenate([jnp.full((BLOCK, 1), sink_ref[h] * LOG2E, F32) for h in heads], axis=0)
        m = jnp.maximum(jnp.max(s, axis=1, keepdims=True), sink)
        p = jnp.exp2(s - m).astype(BF16)
        r = jnp.concatenate(
            [_dot(p[a * BLOCK:(a + 1) * BLOCK, :],
                  v_win[i * BLOCK:(i + 3) * BLOCK, (h // ATT_GROUP) * V7X_LANES:(h // ATT_GROUP + 1) * V7X_LANES])
             for a, h in enumerate(heads)], axis=0)
        o = r[:, :HEAD_DIM] / (r[:, HEAD_DIM:HEAD_DIM + 1] + jnp.exp2(sink - m))
        outs += [o[a * BLOCK:(a + 1) * BLOCK, :] for a in range(SWA_STAGE_HEADS)]
        if len(outs) == ATT_HEADS:
            o_ref[i * BLOCK:(i + 1) * BLOCK, :] = jnp.concatenate(outs, axis=1).astype(BF16)
            outs = []


def _swa(q, kt, v, sink, *, tq):
    b, s, _ = q.shape
    nsub = tq // BLOCK
    nblk = s // BLOCK
    nq = s // tq
    left = lambda j: jnp.maximum(j * nsub - 1, 0)
    right = lambda j: jnp.minimum((j + 1) * nsub, nblk - 1)
    est = 2 * (tq * Q_ATT * 2 * 2 + (tq + 2 * BLOCK) * (KV_ATT + VX_ATT) * 2) + (12 << 20)
    return pl.pallas_call(
        functools.partial(_swa_kernel, seq=s, tq=tq),
        grid=(b, nq),
        in_specs=[
            pl.BlockSpec(memory_space=pltpu.SMEM),
            pl.BlockSpec((None, tq, Q_ATT), lambda i, j: (i, j, 0)),
            pl.BlockSpec((None, None, KV_ATT, BLOCK), lambda i, j: (i, 0, 0, left(j))),
            pl.BlockSpec((None, None, KV_ATT, tq), lambda i, j: (i, 0, 0, j)),
            pl.BlockSpec((None, None, KV_ATT, BLOCK), lambda i, j: (i, 0, 0, right(j))),
            pl.BlockSpec((None, BLOCK, VX_ATT), lambda i, j: (i, left(j), 0)),
            pl.BlockSpec((None, tq, VX_ATT), lambda i, j: (i, j, 0)),
            pl.BlockSpec((None, BLOCK, VX_ATT), lambda i, j: (i, right(j), 0)),
        ],
        out_specs=pl.BlockSpec((None, tq, Q_ATT), lambda i, j: (i, j, 0)),
        out_shape=jax.ShapeDtypeStruct((b, s, Q_ATT), BF16),
        compiler_params=_params(("parallel", "parallel"), est),
        name="swa",
    )(sink, q, kt, kt, kt, v, v, v)


def _axial_kernel(q_ref, kt_ref, v_ref, o_ref, m_ref, acc_ref, s_ref, *, tk, nk):
    q_rows = q_ref[...]
    qs = [q_rows[:, h * HEAD_DIM:(h + 1) * HEAD_DIM] for h in range(ATT_HEADS)]
    m_ref[...] = jnp.full(m_ref.shape, -jnp.inf, F32)
    acc_ref[...] = jnp.zeros_like(acc_ref)

    def scores(h, kt_c):
        g = h // ATT_GROUP
        return _dot(qs[h], kt_c[g * HEAD_DIM:(g + 1) * HEAD_DIM, :])

    kt_0 = kt_ref[0]
    for h in range(AXIAL_AHEAD):
        s_ref[h] = scores(h, kt_0)

    def body(c, carry):
        kt_c = kt_ref[c]
        kt_n = kt_ref[jnp.minimum(c + 1, nk - 1)]
        v_c = v_ref[pl.ds(pl.multiple_of(c * tk, tk), tk), :]
        pending = {h: s_ref[h] for h in range(AXIAL_AHEAD)}
        for h in range(ATT_HEADS):
            ahead = h + AXIAL_AHEAD
            if ahead < ATT_HEADS:
                pending[ahead] = scores(ahead, kt_c)
            else:
                s_ref[ahead - ATT_HEADS] = scores(ahead - ATT_HEADS, kt_n)
            g = h // ATT_GROUP
            s = pending.pop(h)
            m_prev = m_ref[h]
            m_new = jnp.maximum(m_prev, jnp.max(s, axis=1, keepdims=True))
            p = jnp.exp2(s - m_new).astype(BF16)
            acc_ref[h] = jnp.exp2(m_prev - m_new) * acc_ref[h] \
                + _dot(p, v_c[:, g * V7X_LANES:(g + 1) * V7X_LANES])
            m_ref[h] = m_new
        return carry

    lax.fori_loop(0, nk, body, 0)
    outs = []
    for h in range(ATT_HEADS):
        acc = acc_ref[h]
        outs.append(acc[:, :HEAD_DIM] / acc[:, HEAD_DIM:HEAD_DIM + 1])
    o_ref[...] = jnp.concatenate(outs, axis=1).astype(BF16)


def _axial(q, kt, v, *, tq, tk):
    b, s, _ = q.shape
    nk = s // tk
    rows = ATT_GROUP * tq
    est = 2 * tq * Q_ATT * 2 * 2 + s * (KV_ATT + VX_ATT) * 2 \
        + ATT_KV_HEADS * (2 * rows * V7X_LANES * 4 + rows * tk * 6) + (4 << 20)
    return pl.pallas_call(
        functools.partial(_axial_kernel, tk=tk, nk=nk),
        grid=(b, s // tq),
        in_specs=[
            pl.BlockSpec((None, tq, Q_ATT), lambda i, j: (i, j, 0)),
            pl.BlockSpec((None, nk, KV_ATT, tk), lambda i, j: (i, 0, 0, 0), pipeline_mode=pl.Buffered(1)),
            pl.BlockSpec((None, s, VX_ATT), lambda i, j: (i, 0, 0), pipeline_mode=pl.Buffered(1)),
        ],
        out_specs=pl.BlockSpec((None, tq, Q_ATT), lambda i, j: (i, j, 0)),
        out_shape=jax.ShapeDtypeStruct((b, s, Q_ATT), BF16),
        scratch_shapes=[
            pltpu.VMEM((ATT_HEADS, tq, 1), F32), pltpu.VMEM((ATT_HEADS, tq, V7X_LANES), F32),
            pltpu.VMEM((AXIAL_AHEAD, tq, tk), F32),
        ],
        compiler_params=_params(("parallel", "parallel"), est),
        name="axial",
    )(q, kt, v)


def _rope_tables(pos, dim):
    inv = ROPE_THETA ** (-jnp.arange(0, dim, 2, dtype=F32) / dim)
    ang = pos.astype(F32)[:, None] * inv[None, :]
    ang = jnp.concatenate([ang, ang], axis=-1)
    return jnp.cos(ang), jnp.sin(ang)


def _rot_half_perm(widths):
    perm, sign, base = [], [], 0
    for w in widths:
        half = w // 2
        perm += [base + half + i for i in range(half)] + [base + i for i in range(half)]
        sign += [-1.0] * half + [1.0] * half
        base += w
    return jnp.array(perm, jnp.int32), jnp.array(sign, F32)


def _attn_tables(cos, sin):
    reps = V7X_LANES // HEAD_DIM
    return {
        "cos_q": jnp.tile(cos, (1, reps)), "sin_q": jnp.tile(sin, (1, reps)),
        "cos_kt": cos.T, "sin_kt": sin.T,
    }


def _attn_weights(w_in, q_norm, k_norm, widths):
    perm, sign = _rot_half_perm(widths)
    wq = w_in[:, :Q_ATT]
    wk = w_in[:, Q_ATT:Q_ATT + KV_ATT]
    wv = w_in[:, Q_ATT + KV_ATT:]

    def rotated(w, heads):
        w3 = w.reshape(D_MODEL, heads, HEAD_DIM)
        return (w3[:, :, perm] * sign).reshape(D_MODEL, heads * HEAD_DIM)

    seg_id = jnp.arange(Q_ATT) // HEAD_DIM
    reps = V7X_LANES // HEAD_DIM
    return {
        "wq": wq.astype(BF16), "wqr": rotated(wq, ATT_HEADS).astype(BF16),
        "wkt": wk.T.astype(BF16), "wkrt": rotated(wk, ATT_KV_HEADS).T.astype(BF16),
        "wv": jnp.pad(wv.reshape(D_MODEL, ATT_KV_HEADS, HEAD_DIM),
                      ((0, 0), (0, 0), (0, V7X_LANES - HEAD_DIM))).reshape(D_MODEL, VX_ATT).astype(BF16),
        "seg": (seg_id[:, None] == seg_id[None, :]).astype(BF16),
        "qw": jnp.tile(q_norm, reps)[None, :], "qwsw": jnp.tile(q_norm[perm], reps)[None, :],
        "kw": k_norm[:, None], "kwsw": k_norm[perm][:, None],
    }


def _ffn_weights(w13, w2):
    def chunked(w):
        return w.reshape(D_MODEL, N_FFN_CHUNKS, FFN_CHUNK).transpose(1, 0, 2).astype(BF16)
    return chunked(w13[:, :D_FF]), chunked(w13[:, D_FF:]), w2.reshape(N_FFN_CHUNKS, FFN_CHUNK, D_MODEL).astype(BF16)


def _mlstm_weights(w_in, b_gate, norm_w, w_out):
    wq = w_in[:, :QK_A]
    wk = w_in[:, QK_A:2 * QK_A]
    wv = w_in[:, 2 * QK_A:2 * QK_A + V_A]
    wo = w_in[:, 2 * QK_A + V_A:2 * QK_A + 2 * V_A]
    wg = w_in[:, 2 * QK_A + 2 * V_A:]
    return {
        "wq": wq.astype(BF16), "wkt": wk.T.astype(BF16), "wv": wv.astype(BF16), "wo": wo.astype(BF16),
        "wgt": wg.T.astype(BF16), "bgt": b_gate[:, None],
        "mnw": norm_w[None, :], "wout": w_out.astype(BF16),
    }


def _tile(s, want):
    t = min(s, want)
    assert s % t == 0
    return t


def _trunk(x, mod_all, prep):
    b, s, _ = x.shape
    tm_ffn = _tile(s, 1024)
    tm = _tile(s, 512)
    for i in range(DEPTH):
        mod = mod_all[i].reshape(b, N_MOD, D_MODEL)
        lw = prep["layers"][i]
        x = _ffn(x, mod, lw["nw"][0:1], *lw["ffn0"], row0=0, tm=tm_ffn)
        kind = i % N_MIXERS
        mw = lw["mixer"]
        if kind == 0:
            q, kt, v, grow = _mlstm_in(x, mod, lw["nw"][1:2], mw["wq"], mw["wkt"], mw["wv"],
                                       mw["wgt"], mw["bgt"], tm=tm)
            hf, hb = _mlstm_scan(q, kt, v, grow, tb=_tile(s, 512))
            x = _mlstm_out(x, hf, hb, mod, lw["nw"][1:2], mw["wo"], mw["mnw"], mw["wout"], tm=tm)
        elif kind == 1:
            q, kt, v = _attn_in(x, mod, lw["nw"][1:2], mw, prep["swa_tabs"][s], tm=tm, tk=s,
                                q_scale=HEAD_DIM ** -0.5 * LOG2E)
            o = _swa(q, kt, v, mw["sink"], tq=_tile(s, 512))
            x = _attn_out(x, o, mod, mw["wout"], tm=tm)
        else:
            tk = _tile(s, 1024)
            q, kt, v = _attn_in(x, mod, lw["nw"][1:2], mw, prep["axial_tabs"][s], tm=tm, tk=tk,
                                q_scale=HEAD_DIM ** -0.5 * LOG2E)
            o = _axial(q, kt, v, tq=_tile(s, 128), tk=tk)
            x = _attn_out(x, o, mod, mw["wout"], tm=tm)
        x = _ffn(x, mod, lw["nw"][2:3], *lw["ffn1"], row0=6, tm=tm_ffn)
    return x


def _prepare(seqs, ffn_w13, ffn_w2, norm_w,
             mlstm_w_in, mlstm_b_gate, mlstm_norm_w, mlstm_w_out,
             swa_w_in, swa_q_norm, swa_k_norm, swa_sink, swa_w_out,
             axial_w_in, axial_q_norm, axial_k_norm, axial_w_out):
    layers = []
    for i in range(DEPTH):
        kind, j = i % N_MIXERS, i // N_MIXERS
        if kind == 0:
            mixer = _mlstm_weights(mlstm_w_in[j], mlstm_b_gate[j], mlstm_norm_w[j], mlstm_w_out[j])
        elif kind == 1:
            mixer = _attn_weights(swa_w_in[j], swa_q_norm[j], swa_k_norm[j], (HEAD_DIM,))
            mixer["sink"] = swa_sink[j]
            mixer["wout"] = swa_w_out[j].astype(BF16)
        else:
            mixer = _attn_weights(axial_w_in[j], axial_q_norm[j], axial_k_norm[j], (HEAD_DIM // 2, HEAD_DIM // 2))
            mixer["wout"] = axial_w_out[j].astype(BF16)
        layers.append({
            "nw": norm_w[i],
            "ffn0": _ffn_weights(ffn_w13[i, 0], ffn_w2[i, 0]),
            "ffn1": _ffn_weights(ffn_w13[i, 1], ffn_w2[i, 1]),
            "mixer": mixer,
        })
    swa_tabs, axial_tabs = {}, {}
    for s in seqs:
        swa_tabs[s] = _attn_tables(*_rope_tables(jnp.arange(s), HEAD_DIM))
        rows = s // GRID_W
        row_ids = jnp.repeat(jnp.arange(rows), GRID_W)
        col_ids = jnp.tile(jnp.arange(GRID_W), rows)
        rc, rs = _rope_tables(row_ids, HEAD_DIM // 2)
        cc, cs = _rope_tables(col_ids, HEAD_DIM // 2)
        axial_tabs[s] = _attn_tables(jnp.concatenate([rc, cc], axis=-1), jnp.concatenate([rs, cs], axis=-1))
    return {"layers": layers, "swa_tabs": swa_tabs, "axial_tabs": axial_tabs}


def kernel(x_prompt, x_sample, c_prompt, c_sample, ffn_w13, ffn_w2, ada_w, ada_b, norm_w, mlstm_w_in, mlstm_b_gate, mlstm_norm_w, mlstm_w_out, swa_w_in, swa_q_norm, swa_k_norm, swa_sink, swa_w_out, axial_w_in, axial_q_norm, axial_k_norm, axial_w_out):
    prep = _prepare({x_prompt.shape[1], x_sample.shape[1]}, ffn_w13, ffn_w2, norm_w,
                    mlstm_w_in, mlstm_b_gate, mlstm_norm_w, mlstm_w_out,
                    swa_w_in, swa_q_norm, swa_k_norm, swa_sink, swa_w_out,
                    axial_w_in, axial_q_norm, axial_k_norm, axial_w_out)
    nb_prompt = c_prompt.shape[0]
    mod_all = _ada_mod(jnp.concatenate([c_prompt, c_sample], axis=0), ada_w, ada_b)
    y_prompt = _trunk(x_prompt, mod_all[:, :nb_prompt], prep)
    y_sample = _trunk(x_sample, mod_all[:, nb_prompt:], prep)
    return (y_prompt, y_sample)
```

```python
import functools

import jax
import jax.numpy as jnp
from jax import lax
from jax.experimental import pallas as pl
from jax.experimental.pallas import tpu as pltpu

F32 = jnp.float32
BF16 = jnp.bfloat16

D_MODEL = 1024
DEPTH = 4
N_MIXERS = 3
D_FF = 2816
EPS = 1e-6
N_MOD = 9
A_HEADS = 8
A_DK = 64
A_DV = 128
A_CHUNK = 128
A_GATE_CAP = 15.0
ATT_HEADS = 16
ATT_KV_HEADS = 4
ATT_GROUP = ATT_HEADS // ATT_KV_HEADS
HEAD_DIM = 64
WINDOW = 128
BLOCK = 128
ROPE_THETA = 10000.0
GRID_W = 64

V7X_VMEM_BYTES = 64 * 1024 * 1024
V7X_LANES = 128
V7X_MXU_DIM = 256

FFN_CHUNK = V7X_MXU_DIM
N_FFN_CHUNKS = D_FF // FFN_CHUNK
QK_A = A_HEADS * A_DK
V_A = A_HEADS * A_DV
Q_ATT = ATT_HEADS * HEAD_DIM
KV_ATT = ATT_KV_HEADS * HEAD_DIM
VX_ATT = ATT_KV_HEADS * V7X_LANES
LOG2E = 1.4426950408889634
AXIAL_AHEAD = 3
SWA_STAGE_HEADS = 8

NT_DIMS = (((1,), (1,)), ((), ()))


def _vmem_limit(est_bytes):
    return int(min(est_bytes * 5 // 4 + (4 << 20), V7X_VMEM_BYTES - (4 << 20)))


def _params(sem, est_bytes):
    return pltpu.CompilerParams(dimension_semantics=sem, vmem_limit_bytes=_vmem_limit(est_bytes))


def _resident(shape):
    nd = len(shape)
    return pl.BlockSpec(shape, lambda *_: (0,) * nd, pipeline_mode=pl.Buffered(1))


def _dot(a, b):
    return jnp.dot(a, b, preferred_element_type=F32)


def _dot_nt(a, b):
    return lax.dot_general(a, b, NT_DIMS, preferred_element_type=F32)


def _norm_mod(x, nw, shift, scale):
    ms = jnp.mean(x * x, axis=-1, keepdims=True)
    y = x * lax.rsqrt(ms + EPS)
    return (y * nw) * (1.0 + scale) + shift


def _sigmoid(x):
    return 1.0 / (1.0 + jnp.exp(-x))


def _ada_kernel(c_ref, w_ref, b_ref, o_ref):
    c = c_ref[...]
    a = c * _sigmoid(c)
    o_ref[...] = _dot(a, w_ref[...]) + b_ref[...]


def _ada_mod(c_all, ada_w, ada_b):
    bc = c_all.shape[0]
    n_out = N_MOD * D_MODEL
    tn = D_MODEL
    return pl.pallas_call(
        _ada_kernel,
        grid=(DEPTH, n_out // tn),
        in_specs=[
            pl.BlockSpec((bc, D_MODEL), lambda i, n: (0, 0)),
            pl.BlockSpec((None, D_MODEL, tn), lambda i, n: (i, 0, n)),
            pl.BlockSpec((None, 1, tn), lambda i, n: (i, 0, n)),
        ],
        out_specs=pl.BlockSpec((None, bc, tn), lambda i, n: (i, 0, n)),
        out_shape=jax.ShapeDtypeStruct((DEPTH, bc, n_out), F32),
        compiler_params=_params(("parallel", "parallel"), 4 * (2 * D_MODEL * tn + 4 * bc * tn)),
        name="ada_mod",
    )(c_all, ada_w, ada_b.reshape(DEPTH, 1, n_out))


def _ffn_kernel(x_ref, mod_ref, nw_ref, w1_ref, w3_ref, w2_ref, o_ref, h_ref, a_ref, acc_ref, *, row0):
    x = x_ref[...]
    shift = mod_ref[row0:row0 + 1, :]
    scale = mod_ref[row0 + 1:row0 + 2, :]
    gate = mod_ref[row0 + 2:row0 + 3, :]
    h_ref[...] = _norm_mod(x, nw_ref[...], shift, scale).astype(BF16)

    def act(c):
        h = h_ref[...]
        g = _dot(h, w1_ref[c])
        u = _dot(h, w3_ref[c])
        return ((g * _sigmoid(g)) * u).astype(BF16)

    a_ref[0] = act(0)
    a_ref[1] = act(1)
    acc_ref[...] = _dot(a_ref[0], w2_ref[0])

    def body(c, carry):
        a_next = act(c + 1)
        acc_ref[...] += _dot(a_ref[c % 2], w2_ref[c])
        a_ref[(c + 1) % 2] = a_next
        return carry

    lax.fori_loop(1, N_FFN_CHUNKS - 1, body, 0, unroll=3)
    last = N_FFN_CHUNKS - 1
    o_ref[...] = x + (0.5 * gate) * (acc_ref[...] + _dot(a_ref[last % 2], w2_ref[last]))


def _ffn(x, mod, nw, w1, w3, w2, *, row0, tm):
    b, s, _ = x.shape
    est = 4 * tm * D_MODEL * 4 + 3 * D_MODEL * D_FF * 2 + tm * D_MODEL * 6 + 4 * tm * FFN_CHUNK * 4
    return pl.pallas_call(
        functools.partial(_ffn_kernel, row0=row0),
        grid=(b, s // tm),
        in_specs=[
            pl.BlockSpec((None, tm, D_MODEL), lambda i, m: (i, m, 0)),
            pl.BlockSpec((None, N_MOD, D_MODEL), lambda i, m: (i, 0, 0)),
            pl.BlockSpec((1, D_MODEL), lambda i, m: (0, 0)),
            _resident(w1.shape),
            _resident(w3.shape),
            _resident(w2.shape),
        ],
        out_specs=pl.BlockSpec((None, tm, D_MODEL), lambda i, m: (i, m, 0)),
        out_shape=jax.ShapeDtypeStruct(x.shape, F32),
        scratch_shapes=[pltpu.VMEM((tm, D_MODEL), BF16), pltpu.VMEM((2, tm, FFN_CHUNK), BF16),
                        pltpu.VMEM((tm, D_MODEL), F32)],
        compiler_params=_params(("parallel", "parallel"), est),
        name="ffn",
    )(x, mod, nw, w1, w3, w2)


def _log_sigmoid(x):
    return jnp.minimum(x, 0.0) - jnp.log1p(jnp.exp(-jnp.abs(x)))


def _gate_act(g, is_forget):
    g = A_GATE_CAP * jnp.tanh(g / A_GATE_CAP)
    return jnp.where(is_forget, _log_sigmoid(g), g)


def _mlstm_in_kernel(x_ref, mod_ref, nw_ref, wq_ref, wkt_ref, wv_ref, wgt_ref, bgt_ref,
                     q_ref, kt_ref, v_ref, rows_ref, cols_ref):
    L, H = A_CHUNK, A_HEADS
    cpt = kt_ref.shape[0]
    h = _norm_mod(x_ref[...], nw_ref[...], mod_ref[3:4, :], mod_ref[4:5, :]).astype(BF16)
    g_row = _dot_nt(wgt_ref[...], h) + bgt_ref[...]
    q_ref[...] = (_dot(h, wq_ref[...]) * (A_DK ** -0.5)).astype(BF16)

    row_id = lax.broadcasted_iota(jnp.int32, g_row.shape, 0)
    g_row = _gate_act(g_row, (row_id // H) % 2 == 1)
    ri = lax.broadcasted_iota(jnp.int32, (L, L), 0)
    ci = lax.broadcasted_iota(jnp.int32, (L, L), 1)
    tris = (jnp.where(ri <= ci, 1.0, 0.0).astype(BF16), jnp.where(ri >= ci, 1.0, 0.0).astype(BF16))
    eye = jnp.where(ri == ci, 1.0, 0.0).astype(BF16)
    lane = lax.broadcasted_iota(jnp.int32, (cpt * H, L), 1)

    def chunk_rows(first_row):
        return jnp.concatenate([g_row[first_row:first_row + H, j * L:(j + 1) * L] for j in range(cpt)], axis=0)

    igs = [chunk_rows(2 * d * H) for d in range(2)]
    bs = [_cumsum_rows(chunk_rows((2 * d + 1) * H), tris[d]) for d in range(2)]
    v_ref[...] = _dot(h, wv_ref[...]).astype(BF16)

    col_parts = []
    for d in range(2):
        b, ig = bs[d], igs[d]
        z = b - ig
        cm = _cummax_lanes(-z, lane, reverse=(d == 1))
        b_last = b[:, L - 1:L] if d == 0 else b[:, 0:1]
        log_w = (b_last - b) + ig
        lw_max = jnp.broadcast_to(jnp.max(log_w, axis=1, keepdims=True), b.shape)
        b_last = jnp.broadcast_to(b_last, b.shape)
        for j in range(cpt):
            rows_ref[j, d] = jnp.concatenate([a[j * H:(j + 1) * H, :] for a in (z, log_w, lw_max, b_last)], axis=0)
        col_parts.append((cm, b))
    kt = _dot_nt(wkt_ref[...], h).astype(BF16)
    for j in range(cpt):
        kt_ref[j] = kt[:, j * L:(j + 1) * L]
        packed = jnp.concatenate([a[j * H:(j + 1) * H, :] for d in range(2) for a in col_parts[d]]
                                 + [jnp.zeros((L - 4 * H, L), F32)], axis=0)
        cols_ref[j] = _transpose_rows(eye, packed)


def _mlstm_in(x, mod, nw, wq, wkt, wv, wgt, bgt, *, tm):
    b, s, _ = x.shape
    nch, cpt = s // A_CHUNK, tm // A_CHUNK
    ng = 4 * A_HEADS
    est = 2 * tm * D_MODEL * 4 + 2 * (D_MODEL * (2 * QK_A + V_A + V7X_LANES)) * 2 \
        + 2 * tm * (QK_A * 2 * 2 + V_A * 2 + 2 * V7X_LANES * 4) + tm * (2 * QK_A + V_A) * 4 + (4 << 20)
    return pl.pallas_call(
        _mlstm_in_kernel,
        grid=(b, s // tm),
        in_specs=[
            pl.BlockSpec((None, tm, D_MODEL), lambda i, m: (i, m, 0)),
            pl.BlockSpec((None, N_MOD, D_MODEL), lambda i, m: (i, 0, 0)),
            pl.BlockSpec((1, D_MODEL), lambda i, m: (0, 0)),
            _resident(wq.shape), _resident(wkt.shape), _resident(wv.shape),
            _resident(wgt.shape), _resident(bgt.shape),
        ],
        out_specs=[
            pl.BlockSpec((None, tm, QK_A), lambda i, m: (i, m, 0)),
            pl.BlockSpec((None, cpt, QK_A, A_CHUNK), lambda i, m: (i, m, 0, 0)),
            pl.BlockSpec((None, tm, V_A), lambda i, m: (i, m, 0)),
            pl.BlockSpec((None, cpt, 2, ng, A_CHUNK), lambda i, m: (i, m, 0, 0, 0)),
            pl.BlockSpec((None, cpt, A_CHUNK, A_CHUNK), lambda i, m: (i, m, 0, 0)),
        ],
        out_shape=[
            jax.ShapeDtypeStruct((b, s, QK_A), BF16),
            jax.ShapeDtypeStruct((b, nch, QK_A, A_CHUNK), BF16),
            jax.ShapeDtypeStruct((b, s, V_A), BF16),
            jax.ShapeDtypeStruct((b, nch, 2, ng, A_CHUNK), F32),
            jax.ShapeDtypeStruct((b, nch, A_CHUNK, A_CHUNK), F32),
        ],
        compiler_params=_params(("parallel", "parallel"), est),
        name="mlstm_in",
    )(x, mod, nw, wq, wkt, wv, wgt, bgt)


def _split3(a):
    hi = a.astype(BF16)
    r1 = a - hi.astype(F32)
    mid = r1.astype(BF16)
    lo = (r1 - mid.astype(F32)).astype(BF16)
    return hi, mid, lo


def _cumsum_rows(a, tri):
    hi, mid, lo = _split3(a)
    return _dot(hi, tri) + _dot(mid, tri) + _dot(lo, tri)


def _transpose_rows(eye, a):
    hi, mid, lo = _split3(a)
    return _dot_nt(eye, hi) + _dot_nt(eye, mid) + _dot_nt(eye, lo)


def _cummax_lanes(x, lane, reverse):
    n = x.shape[1]
    k = 1
    while k < n:
        if reverse:
            shifted = jnp.where(lane < n - k, pltpu.roll(x, n - k, axis=1), -jnp.inf)
        else:
            shifted = jnp.where(lane >= k, pltpu.roll(x, k, axis=1), -jnp.inf)
        x = jnp.maximum(x, shifted)
        k *= 2
    return x


def _mlstm_scan_kernel(qf_ref, ktf_ref, vf_ref, rowsf_ref, colsf_ref, qb_ref, ktb_ref, vb_ref, rowsb_ref, colsb_ref,
                       hf_ref, hb_ref, c_ref, m_ref, *, cps):
    L = A_CHUNK
    H = A_HEADS

    @pl.when(pl.program_id(1) == 0)
    def _():
        c_ref[...] = jnp.zeros_like(c_ref)
        m_ref[...] = jnp.full(m_ref.shape, -jnp.inf, F32)

    ri = lax.broadcasted_iota(jnp.int32, (L, L), 0)
    ci = lax.broadcasted_iota(jnp.int32, (L, L), 1)
    keeps = (ri >= ci, ri <= ci)
    ones_blk = jnp.ones((L, L), BF16)
    refs = ((qf_ref, ktf_ref, vf_ref, rowsf_ref, colsf_ref, hf_ref),
            (qb_ref, ktb_ref, vb_ref, rowsb_ref, colsb_ref, hb_ref))

    def body(c, carry):
        stages = []
        for d in range(2):
            q_ref, kt_ref, v_ref, rows_ref, cols_ref, out_ref = refs[d]
            cc = c if d == 0 else cps - 1 - c
            r0 = pl.multiple_of(cc * L, L)
            rows = rows_ref[cc, d]
            z, log_w, lw_max, b_last = (rows[k * H:(k + 1) * H, :] for k in range(4))
            cols = cols_ref[cc]
            m_prev = m_ref[d * H:(d + 1) * H, :]
            m_new = jnp.maximum(b_last + m_prev, lw_max)
            w = jnp.exp(log_w - m_new)
            decay = jnp.exp((b_last + m_prev) - m_new)
            m_ref[d * H:(d + 1) * H, :] = m_new
            q_all = q_ref[pl.ds(r0, L), :]
            kt_all = kt_ref[cc]
            for hd in range(H):
                stages.append(dict(
                    d=d, sd=d * H + hd, out_ref=out_ref, r0=r0, hd=hd, cols=cols,
                    q=q_all[:, hd * A_DK:(hd + 1) * A_DK], kt=kt_all[hd * A_DK:(hd + 1) * A_DK, :],
                    v_ext=jnp.concatenate([v_ref[pl.ds(r0, L), hd * A_DV:(hd + 1) * A_DV], ones_blk], axis=1),
                    z=z[hd:hd + 1, :], m_prev=m_prev[hd:hd + 1, :], w=w[hd:hd + 1, :],
                    decay=decay[hd:hd + 1, 0:1]))
        for st in stages:
            st["c_prev"] = c_ref[st["sd"]]
            st["qk"] = _dot(st["q"], st["kt"])
            st["qc"] = _dot(st["q"], st["c_prev"].astype(BF16))
        for st in stages:
            lane = 2 * st["d"] * H + st["hd"]
            cm_b = jnp.take_along_axis(st["cols"], jnp.full((L, L), lane, jnp.int32), axis=1)
            b_b = jnp.take_along_axis(st["cols"], jnp.full((L, L), lane + H, jnp.int32), axis=1)
            u_b = -jnp.maximum(st["m_prev"], cm_b)
            st["dmat"] = jnp.exp(jnp.where(keeps[st["d"]], u_b - st["z"], -jnp.inf))
            w_inter = jnp.exp(u_b + st["m_prev"])
            st["w_inter"] = jnp.concatenate([w_inter, w_inter], axis=1)
            st["floor"] = jnp.exp(u_b - b_b)
        for st in stages:
            st["sv"] = _dot((st["qk"] * st["dmat"]).astype(BF16), st["v_ext"])
            kw = (st["kt"].astype(F32) * st["w"]).astype(BF16)
            st["upd"] = _dot(kw, st["v_ext"])
        for st in stages:
            r = st["sv"] + st["w_inter"] * st["qc"]
            floor = st["floor"]
            hd = st["hd"]
            st["out_ref"][pl.ds(st["r0"], L), hd * A_DV:(hd + 1) * A_DV] = \
                r[:, :L] / jnp.maximum(jnp.abs(r[:, L:]), floor)
            c_ref[st["sd"]] = st["decay"] * st["c_prev"] + st["upd"]
        return carry

    lax.fori_loop(0, cps, body, 0)


def _mlstm_scan(q, kt, v, rows, cols, *, tb):
    b, s, _ = q.shape
    cps = tb // A_CHUNK
    nb = s // tb
    ng = 4 * A_HEADS

    def specs(idx):
        return [
            pl.BlockSpec((None, tb, QK_A), lambda i, j: (i, idx(j), 0)),
            pl.BlockSpec((None, cps, QK_A, A_CHUNK), lambda i, j: (i, idx(j), 0, 0)),
            pl.BlockSpec((None, tb, V_A), lambda i, j: (i, idx(j), 0)),
            pl.BlockSpec((None, cps, 2, ng, A_CHUNK), lambda i, j: (i, idx(j), 0, 0, 0)),
            pl.BlockSpec((None, cps, A_CHUNK, A_CHUNK), lambda i, j: (i, idx(j), 0, 0)),
        ]

    fwd = lambda j: j
    bwd = lambda j: nb - 1 - j
    est = 2 * 2 * tb * (QK_A * 2 * 2 + V_A * 2 + V7X_LANES * 4) + 2 * 2 * tb * V_A * 4 \
        + 2 * A_HEADS * A_DK * 2 * A_CHUNK * 4 + (16 << 20)
    return pl.pallas_call(
        functools.partial(_mlstm_scan_kernel, cps=cps),
        grid=(b, nb),
        in_specs=specs(fwd) + specs(bwd),
        out_specs=[
            pl.BlockSpec((None, tb, V_A), lambda i, j: (i, j, 0)),
            pl.BlockSpec((None, tb, V_A), lambda i, j: (i, nb - 1 - j, 0)),
        ],
        out_shape=[jax.ShapeDtypeStruct((b, s, V_A), F32)] * 2,
        scratch_shapes=[
            pltpu.VMEM((2 * A_HEADS, A_DK, 2 * A_CHUNK), F32),
            pltpu.VMEM((2 * A_HEADS, A_CHUNK), F32),
        ],
        compiler_params=_params(("parallel", "arbitrary"), est),
        name="mlstm_scan",
    )(q, kt, v, rows, cols, q, kt, v, rows, cols)


def _mlstm_out_kernel(x_ref, hf_ref, hb_ref, mod_ref, nw_ref, wo_ref, mnw_ref, wout_ref, o_ref):
    x = x_ref[...]
    h = _norm_mod(x, nw_ref[...], mod_ref[3:4, :], mod_ref[4:5, :]).astype(BF16)
    o_gate = _dot(h, wo_ref[...])
    hs = hf_ref[...] + hb_ref[...]
    parts = []
    for hd in range(A_HEADS):
        a = hs[:, hd * A_DV:(hd + 1) * A_DV]
        ms = jnp.mean(a * a, axis=-1, keepdims=True)
        parts.append(a * lax.rsqrt(ms + EPS))
    y = jnp.concatenate(parts, axis=1) * mnw_ref[...]
    z = (y * _sigmoid(o_gate)).astype(BF16)
    o_ref[...] = x + mod_ref[5:6, :] * _dot(z, wout_ref[...])


def _mlstm_out(x, hf, hb, mod, nw, wo, mnw, wout, *, tm):
    b, s, _ = x.shape
    tile = pl.BlockSpec((None, tm, D_MODEL), lambda i, m: (i, m, 0))
    est = 2 * 4 * tm * D_MODEL * 4 + 2 * D_MODEL * D_MODEL * 2 + 6 * tm * D_MODEL * 4
    return pl.pallas_call(
        _mlstm_out_kernel,
        grid=(b, s // tm),
        in_specs=[
            tile, tile, tile,
            pl.BlockSpec((None, N_MOD, D_MODEL), lambda i, m: (i, 0, 0)),
            pl.BlockSpec((1, D_MODEL), lambda i, m: (0, 0)),
            _resident(wo.shape),
            pl.BlockSpec((1, V_A), lambda i, m: (0, 0)),
            _resident(wout.shape),
        ],
        out_specs=tile,
        out_shape=jax.ShapeDtypeStruct(x.shape, F32),
        compiler_params=_params(("parallel", "parallel"), est),
        name="mlstm_out",
    )(x, hf, hb, mod, nw, wo, mnw, wout)


def _attn_in_kernel(x_ref, mod_ref, nw_ref, wq_ref, wqr_ref, wkt_ref, wkrt_ref, wv_ref, seg_ref,
                    cq_ref, sq_ref, ckt_ref, skt_ref, qw_ref, qwsw_ref, kw_ref, kwsw_ref,
                    q_ref, kt_ref, v_ref, *, q_scale):
    h = _norm_mod(x_ref[...], nw_ref[...], mod_ref[3:4, :], mod_ref[4:5, :]).astype(BF16)
    v = _dot(h, wv_ref[...])
    lane = lax.broadcasted_iota(jnp.int32, v.shape, 1)
    v_ref[...] = jnp.where(lane % V7X_LANES == HEAD_DIM, 1.0, v).astype(BF16)

    q = _dot(h, wq_ref[...])
    q_rot = _dot(h, wqr_ref[...])
    q2 = q * q
    q2_hi = q2.astype(BF16)
    q2_lo = (q2 - q2_hi.astype(F32)).astype(BF16)
    seg = seg_ref[...]
    ssq = _dot(q2_hi, seg) + _dot(q2_lo, seg)
    rq = lax.rsqrt(ssq * (1.0 / HEAD_DIM) + EPS)
    reps = Q_ATT // V7X_LANES
    cos_q = jnp.concatenate([cq_ref[...] * qw_ref[...]] * reps, axis=1)
    sin_q = jnp.concatenate([sq_ref[...] * qwsw_ref[...]] * reps, axis=1)
    q_ref[...] = ((rq * q_scale) * (q * cos_q + q_rot * sin_q)).astype(BF16)

    kt = _dot_nt(wkt_ref[...], h)
    kt_rot = _dot_nt(wkrt_ref[...], h)
    cos_k = ckt_ref[...] * kw_ref[...]
    sin_k = skt_ref[...] * kwsw_ref[...]
    outs = []
    for j in range(ATT_KV_HEADS):
        a = kt[j * HEAD_DIM:(j + 1) * HEAD_DIM, :]
        ar = kt_rot[j * HEAD_DIM:(j + 1) * HEAD_DIM, :]
        rk = lax.rsqrt(jnp.mean(a * a, axis=0, keepdims=True) + EPS)
        outs.append(rk * (a * cos_k + ar * sin_k))
    kt_ref[...] = jnp.concatenate(outs, axis=0).astype(BF16)


def _attn_in(x, mod, nw, w, tabs, *, tm, tk, q_scale):
    b, s, _ = x.shape
    per = tk // tm
    const2 = lambda i, m: (0, 0)
    est = 2 * tm * D_MODEL * 4 + (3 * D_MODEL * Q_ATT + 3 * D_MODEL * KV_ATT) * 2 \
        + 2 * tm * (Q_ATT + 2 * KV_ATT) * 2 + 8 * tm * Q_ATT * 4 + 8 * tm * V7X_LANES * 4
    return pl.pallas_call(
        functools.partial(_attn_in_kernel, q_scale=q_scale),
        grid=(b, s // tm),
        in_specs=[
            pl.BlockSpec((None, tm, D_MODEL), lambda i, m: (i, m, 0)),
            pl.BlockSpec((None, N_MOD, D_MODEL), lambda i, m: (i, 0, 0)),
            pl.BlockSpec((1, D_MODEL), const2),
            _resident(w["wq"].shape), _resident(w["wqr"].shape),
            _resident(w["wkt"].shape), _resident(w["wkrt"].shape),
            _resident(w["wv"].shape), _resident(w["seg"].shape),
            pl.BlockSpec((tm, V7X_LANES), lambda i, m: (m, 0)),
            pl.BlockSpec((tm, V7X_LANES), lambda i, m: (m, 0)),
            pl.BlockSpec((HEAD_DIM, tm), lambda i, m: (0, m)),
            pl.BlockSpec((HEAD_DIM, tm), lambda i, m: (0, m)),
            pl.BlockSpec((1, V7X_LANES), const2), pl.BlockSpec((1, V7X_LANES), const2),
            pl.BlockSpec((HEAD_DIM, 1), const2), pl.BlockSpec((HEAD_DIM, 1), const2),
        ],
        out_specs=[
            pl.BlockSpec((None, tm, Q_ATT), lambda i, m: (i, m, 0)),
            pl.BlockSpec((None, None, KV_ATT, tm), lambda i, m: (i, m // per, 0, m % per)),
            pl.BlockSpec((None, tm, VX_ATT), lambda i, m: (i, m, 0)),
        ],
        out_shape=[
            jax.ShapeDtypeStruct((b, s, Q_ATT), BF16),
            jax.ShapeDtypeStruct((b, s // tk, KV_ATT, tk), BF16),
            jax.ShapeDtypeStruct((b, s, VX_ATT), BF16),
        ],
        compiler_params=_params(("parallel", "parallel"), est),
        name="attn_in",
    )(x, mod, nw, w["wq"], w["wqr"], w["wkt"], w["wkrt"], w["wv"], w["seg"],
      tabs["cos_q"], tabs["sin_q"], tabs["cos_kt"], tabs["sin_kt"],
      w["qw"], w["qwsw"], w["kw"], w["kwsw"])


def _attn_out_kernel(x_ref, o_ref_in, mod_ref, wout_ref, o_ref):
    o_ref[...] = x_ref[...] + mod_ref[5:6, :] * _dot(o_ref_in[...], wout_ref[...])


def _attn_out(x, o, mod, wout, *, tm):
    b, s, _ = x.shape
    tile = pl.BlockSpec((None, tm, D_MODEL), lambda i, m: (i, m, 0))
    est = 2 * tm * D_MODEL * (4 + 2 + 4) + D_MODEL * D_MODEL * 2 + 2 * tm * D_MODEL * 4
    return pl.pallas_call(
        _attn_out_kernel,
        grid=(b, s // tm),
        in_specs=[tile, tile, pl.BlockSpec((None, N_MOD, D_MODEL), lambda i, m: (i, 0, 0)), _resident(wout.shape)],
        out_specs=tile,
        out_shape=jax.ShapeDtypeStruct(x.shape, F32),
        compiler_params=_params(("parallel", "parallel"), est),
        name="attn_out",
    )(x, o, mod, wout)


def _stack_heads(q_rows, g):
    base = g * ATT_GROUP * HEAD_DIM
    return jnp.concatenate(
        [q_rows[:, base + a * HEAD_DIM:base + (a + 1) * HEAD_DIM] for a in range(ATT_GROUP)], axis=0)


def _unstack_heads(o, rows):
    return jnp.concatenate([o[a * rows:(a + 1) * rows, :] for a in range(ATT_GROUP)], axis=1)


def _swa_kernel(sink_ref, q_ref, ktl_ref, ktm_ref, ktr_ref, vl_ref, vm_ref, vr_ref, o_ref, *, seq, tq):
    nsub = tq // BLOCK
    j = pl.program_id(1)
    kt_win = jnp.concatenate([ktl_ref[...], ktm_ref[...], ktr_ref[...]], axis=1)
    v_win = jnp.concatenate([vl_ref[...], vm_ref[...], vr_ref[...]], axis=0)
    qi = lax.broadcasted_iota(jnp.int32, (BLOCK, 3 * BLOCK), 0)
    kj = lax.broadcasted_iota(jnp.int32, (BLOCK, 3 * BLOCK), 1) - BLOCK
    band_bias = jnp.where(jnp.abs(qi - kj) <= WINDOW, 0.0, -jnp.inf)
    biases = []
    for i in range(nsub):
        bias = band_bias
        if i == 0:
            bias = jnp.where(kj + j * tq >= 0, bias, -jnp.inf)
        if i == nsub - 1:
            bias = jnp.where(kj + (j * tq + i * BLOCK) < seq, bias, -jnp.inf)
        biases.append(bias)
    q_tiles = [q_ref[i * BLOCK:(i + 1) * BLOCK, :] for i in range(nsub)]

    per_blk = ATT_HEADS // SWA_STAGE_HEADS
    n_stage = nsub * per_blk

    def heads_of(n):
        i, part = divmod(n, per_blk)
        return i, range(part * SWA_STAGE_HEADS, (part + 1) * SWA_STAGE_HEADS)

    def scores(n):
        i, heads = heads_of(n)
        return jnp.concatenate(
            [_dot(q_tiles[i][:, h * HEAD_DIM:(h + 1) * HEAD_DIM],
                  kt_win[(h // ATT_GROUP) * HEAD_DIM:(h // ATT_GROUP + 1) * HEAD_DIM, i * BLOCK:(i + 3) * BLOCK])
             + biases[i] for h in heads], axis=0)

    s_next = scores(0)
    outs = []
    for n in range(n_stage):
        s = s_next
        if n + 1 < n_stage:
            s_next = scores(n + 1)
        i, heads = heads_of(n)
        sink = jnp.concatenate([jnp.full((BLOCK, 1), sink_ref[h] * LOG2E, F32) for h in heads], axis=0)
        m = jnp.maximum(jnp.max(s, axis=1, keepdims=True), sink)
        p = jnp.exp2(s - m).astype(BF16)
        r = jnp.concatenate(
            [_dot(p[a * BLOCK:(a + 1) * BLOCK, :],
                  v_win[i * BLOCK:(i + 3) * BLOCK, (h // ATT_GROUP) * V7X_LANES:(h // ATT_GROUP + 1) * V7X_LANES])
             for a, h in enumerate(heads)], axis=0)
        o = r[:, :HEAD_DIM] / (r[:, HEAD_DIM:HEAD_DIM + 1] + jnp.exp2(sink - m))
        outs += [o[a * BLOCK:(a + 1) * BLOCK, :] for a in range(SWA_STAGE_HEADS)]
        if len(outs) == ATT_HEADS:
            o_ref[i * BLOCK:(i + 1) * BLOCK, :] = jnp.concatenate(outs, axis=1).astype(BF16)
            outs = []


def _swa(q, kt, v, sink, *, tq):
    b, s, _ = q.shape
    nsub = tq // BLOCK
    nblk = s // BLOCK
    nq = s // tq
    left = lambda j: jnp.maximum(j * nsub - 1, 0)
    right = lambda j: jnp.minimum((j + 1) * nsub, nblk - 1)
    est = 2 * (tq * Q_ATT * 2 * 2 + (tq + 2 * BLOCK) * (KV_ATT + VX_ATT) * 2) + (12 << 20)
    return pl.pallas_call(
        functools.partial(_swa_kernel, seq=s, tq=tq),
        grid=(b, nq),
        in_specs=[
            pl.BlockSpec(memory_space=pltpu.SMEM),
            pl.BlockSpec((None, tq, Q_ATT), lambda i, j: (i, j, 0)),
            pl.BlockSpec((None, None, KV_ATT, BLOCK), lambda i, j: (i, 0, 0, left(j))),
            pl.BlockSpec((None, None, KV_ATT, tq), lambda i, j: (i, 0, 0, j)),
            pl.BlockSpec((None, None, KV_ATT, BLOCK), lambda i, j: (i, 0, 0, right(j))),
            pl.BlockSpec((None, BLOCK, VX_ATT), lambda i, j: (i, left(j), 0)),
            pl.BlockSpec((None, tq, VX_ATT), lambda i, j: (i, j, 0)),
            pl.BlockSpec((None, BLOCK, VX_ATT), lambda i, j: (i, right(j), 0)),
        ],
        out_specs=pl.BlockSpec((None, tq, Q_ATT), lambda i, j: (i, j, 0)),
        out_shape=jax.ShapeDtypeStruct((b, s, Q_ATT), BF16),
        compiler_params=_params(("parallel", "parallel"), est),
        name="swa",
    )(sink, q, kt, kt, kt, v, v, v)


def _axial_kernel(q_ref, kt_ref, v_ref, o_ref, m_ref, acc_ref, s_ref, *, tk, nk):
    q_rows = q_ref[...]
    qs = [q_rows[:, h * HEAD_DIM:(h + 1) * HEAD_DIM] for h in range(ATT_HEADS)]
    m_ref[...] = jnp.full(m_ref.shape, -jnp.inf, F32)
    acc_ref[...] = jnp.zeros_like(acc_ref)

    def scores(h, kt_c):
        g = h // ATT_GROUP
        return _dot(qs[h], kt_c[g * HEAD_DIM:(g + 1) * HEAD_DIM, :])

    kt_0 = kt_ref[0]
    for h in range(AXIAL_AHEAD):
        s_ref[h] = scores(h, kt_0)

    def body(c, carry):
        kt_c = kt_ref[c]
        kt_n = kt_ref[jnp.minimum(c + 1, nk - 1)]
        v_c = v_ref[pl.ds(pl.multiple_of(c * tk, tk), tk), :]
        pending = {h: s_ref[h] for h in range(AXIAL_AHEAD)}
        for h in range(ATT_HEADS):
            ahead = h + AXIAL_AHEAD
            if ahead < ATT_HEADS:
                pending[ahead] = scores(ahead, kt_c)
            else:
                s_ref[ahead - ATT_HEADS] = scores(ahead - ATT_HEADS, kt_n)
            g = h // ATT_GROUP
            s = pending.pop(h)
            m_prev = m_ref[h]
            m_new = jnp.maximum(m_prev, jnp.max(s, axis=1, keepdims=True))
            p = jnp.exp2(s - m_new).astype(BF16)
            acc_ref[h] = jnp.exp2(m_prev - m_new) * acc_ref[h] \
                + _dot(p, v_c[:, g * V7X_LANES:(g + 1) * V7X_LANES])
            m_ref[h] = m_new
        return carry

    lax.fori_loop(0, nk, body, 0)
    outs = []
    for h in range(ATT_HEADS):
        acc = acc_ref[h]
        outs.append(acc[:, :HEAD_DIM] / acc[:, HEAD_DIM:HEAD_DIM + 1])
    o_ref[...] = jnp.concatenate(outs, axis=1).astype(BF16)


def _axial(q, kt, v, *, tq, tk):
    b, s, _ = q.shape
    nk = s // tk
    rows = ATT_GROUP * tq
    est = 2 * tq * Q_ATT * 2 * 2 + s * (KV_ATT + VX_ATT) * 2 \
        + ATT_KV_HEADS * (2 * rows * V7X_LANES * 4 + rows * tk * 6) + (4 << 20)
    return pl.pallas_call(
        functools.partial(_axial_kernel, tk=tk, nk=nk),
        grid=(b, s // tq),
        in_specs=[
            pl.BlockSpec((None, tq, Q_ATT), lambda i, j: (i, j, 0)),
            pl.BlockSpec((None, nk, KV_ATT, tk), lambda i, j: (i, 0, 0, 0), pipeline_mode=pl.Buffered(1)),
            pl.BlockSpec((None, s, VX_ATT), lambda i, j: (i, 0, 0), pipeline_mode=pl.Buffered(1)),
        ],
        out_specs=pl.BlockSpec((None, tq, Q_ATT), lambda i, j: (i, j, 0)),
        out_shape=jax.ShapeDtypeStruct((b, s, Q_ATT), BF16),
        scratch_shapes=[
            pltpu.VMEM((ATT_HEADS, tq, 1), F32), pltpu.VMEM((ATT_HEADS, tq, V7X_LANES), F32),
            pltpu.VMEM((AXIAL_AHEAD, tq, tk), F32),
        ],
        compiler_params=_params(("parallel", "parallel"), est),
        name="axial",
    )(q, kt, v)


def _rope_tables(pos, dim):
    inv = ROPE_THETA ** (-jnp.arange(0, dim, 2, dtype=F32) / dim)
    ang = pos.astype(F32)[:, None] * inv[None, :]
    ang = jnp.concatenate([ang, ang], axis=-1)
    return jnp.cos(ang), jnp.sin(ang)


def _rot_half_perm(widths):
    perm, sign, base = [], [], 0
    for w in widths:
        half = w // 2
        perm += [base + half + i for i in range(half)] + [base + i for i in range(half)]
        sign += [-1.0] * half + [1.0] * half
        base += w
    return jnp.array(perm, jnp.int32), jnp.array(sign, F32)


def _attn_tables(cos, sin):
    reps = V7X_LANES // HEAD_DIM
    return {
        "cos_q": jnp.tile(cos, (1, reps)), "sin_q": jnp.tile(sin, (1, reps)),
        "cos_kt": cos.T, "sin_kt": sin.T,
    }


def _attn_weights(w_in, q_norm, k_norm, widths):
    perm, sign = _rot_half_perm(widths)
    wq = w_in[:, :Q_ATT]
    wk = w_in[:, Q_ATT:Q_ATT + KV_ATT]
    wv = w_in[:, Q_ATT + KV_ATT:]

    def rotated(w, heads):
        w3 = w.reshape(D_MODEL, heads, HEAD_DIM)
        return (w3[:, :, perm] * sign).reshape(D_MODEL, heads * HEAD_DIM)

    seg_id = jnp.arange(Q_ATT) // HEAD_DIM
    reps = V7X_LANES // HEAD_DIM
    return {
        "wq": wq.astype(BF16), "wqr": rotated(wq, ATT_HEADS).astype(BF16),
        "wkt": wk.T.astype(BF16), "wkrt": rotated(wk, ATT_KV_HEADS).T.astype(BF16),
        "wv": jnp.pad(wv.reshape(D_MODEL, ATT_KV_HEADS, HEAD_DIM),
                      ((0, 0), (0, 0), (0, V7X_LANES - HEAD_DIM))).reshape(D_MODEL, VX_ATT).astype(BF16),
        "seg": (seg_id[:, None] == seg_id[None, :]).astype(BF16),
        "qw": jnp.tile(q_norm, reps)[None, :], "qwsw": jnp.tile(q_norm[perm], reps)[None, :],
        "kw": k_norm[:, None], "kwsw": k_norm[perm][:, None],
    }


def _ffn_weights(w13, w2):
    def chunked(w):
        return w.reshape(D_MODEL, N_FFN_CHUNKS, FFN_CHUNK).transpose(1, 0, 2).astype(BF16)
    return chunked(w13[:, :D_FF]), chunked(w13[:, D_FF:]), w2.reshape(N_FFN_CHUNKS, FFN_CHUNK, D_MODEL).astype(BF16)


def _mlstm_weights(w_in, b_gate, norm_w, w_out):
    wq = w_in[:, :QK_A]
    wk = w_in[:, QK_A:2 * QK_A]
    wv = w_in[:, 2 * QK_A:2 * QK_A + V_A]
    wo = w_in[:, 2 * QK_A + V_A:2 * QK_A + 2 * V_A]
    wg = w_in[:, 2 * QK_A + 2 * V_A:]
    return {
        "wq": wq.astype(BF16), "wkt": wk.T.astype(BF16), "wv": wv.astype(BF16), "wo": wo.astype(BF16),
        "wgt": wg.T.astype(BF16), "bgt": b_gate[:, None],
        "mnw": norm_w[None, :], "wout": w_out.astype(BF16),
    }


def _tile(s, want):
    t = min(s, want)
    assert s % t == 0
    return t


def _trunk(x, mod_all, prep):
    b, s, _ = x.shape
    tm_ffn = _tile(s, 1024)
    tm = _tile(s, 512)
    for i in range(DEPTH):
        mod = mod_all[i].reshape(b, N_MOD, D_MODEL)
        lw = prep["layers"][i]
        x = _ffn(x, mod, lw["nw"][0:1], *lw["ffn0"], row0=0, tm=tm_ffn)
        kind = i % N_MIXERS
        mw = lw["mixer"]
        if kind == 0:
            q, kt, v, rows, cols = _mlstm_in(x, mod, lw["nw"][1:2], mw["wq"], mw["wkt"], mw["wv"],
                                             mw["wgt"], mw["bgt"], tm=tm)
            hf, hb = _mlstm_scan(q, kt, v, rows, cols, tb=_tile(s, 512))
            x = _mlstm_out(x, hf, hb, mod, lw["nw"][1:2], mw["wo"], mw["mnw"], mw["wout"], tm=tm)
        elif kind == 1:
            q, kt, v = _attn_in(x, mod, lw["nw"][1:2], mw, prep["swa_tabs"][s], tm=tm, tk=s,
                                q_scale=HEAD_DIM ** -0.5 * LOG2E)
            o = _swa(q, kt, v, mw["sink"], tq=_tile(s, 512))
            x = _attn_out(x, o, mod, mw["wout"], tm=tm)
        else:
            tk = _tile(s, 1024)
            q, kt, v = _attn_in(x, mod, lw["nw"][1:2], mw, prep["axial_tabs"][s], tm=tm, tk=tk,
                                q_scale=HEAD_DIM ** -0.5 * LOG2E)
            o = _axial(q, kt, v, tq=_tile(s, 128), tk=tk)
            x = _attn_out(x, o, mod, mw["wout"], tm=tm)
        x = _ffn(x, mod, lw["nw"][2:3], *lw["ffn1"], row0=6, tm=tm_ffn)
    return x


def _prepare(seqs, ffn_w13, ffn_w2, norm_w,
             mlstm_w_in, mlstm_b_gate, mlstm_norm_w, mlstm_w_out,
             swa_w_in, swa_q_norm, swa_k_norm, swa_sink, swa_w_out,
             axial_w_in, axial_q_norm, axial_k_norm, axial_w_out):
    layers = []
    for i in range(DEPTH):
        kind, j = i % N_MIXERS, i // N_MIXERS
        if kind == 0:
            mixer = _mlstm_weights(mlstm_w_in[j], mlstm_b_gate[j], mlstm_norm_w[j], mlstm_w_out[j])
        elif kind == 1:
            mixer = _attn_weights(swa_w_in[j], swa_q_norm[j], swa_k_norm[j], (HEAD_DIM,))
            mixer["sink"] = swa_sink[j]
            mixer["wout"] = swa_w_out[j].astype(BF16)
        else:
            mixer = _attn_weights(axial_w_in[j], axial_q_norm[j], axial_k_norm[j], (HEAD_DIM // 2, HEAD_DIM // 2))
            mixer["wout"] = axial_w_out[j].astype(BF16)
        layers.append({
            "nw": norm_w[i],
            "ffn0": _ffn_weights(ffn_w13[i, 0], ffn_w2[i, 0]),
            "ffn1": _ffn_weights(ffn_w13[i, 1], ffn_w2[i, 1]),
            "mixer": mixer,
        })
    swa_tabs, axial_tabs = {}, {}
    for s in seqs:
        swa_tabs[s] = _attn_tables(*_rope_tables(jnp.arange(s), HEAD_DIM))
        rows = s // GRID_W
        row_ids = jnp.repeat(jnp.arange(rows), GRID_W)
        col_ids = jnp.tile(jnp.arange(GRID_W), rows)
        rc, rs = _rope_tables(row_ids, HEAD_DIM // 2)
        cc, cs = _rope_tables(col_ids, HEAD_DIM // 2)
        axial_tabs[s] = _attn_tables(jnp.concatenate([rc, cc], axis=-1), jnp.concatenate([rs, cs], axis=-1))
    return {"layers": layers, "swa_tabs": swa_tabs, "axial_tabs": axial_tabs}


def kernel(x_prompt, x_sample, c_prompt, c_sample, ffn_w13, ffn_w2, ada_w, ada_b, norm_w, mlstm_w_in, mlstm_b_gate, mlstm_norm_w, mlstm_w_out, swa_w_in, swa_q_norm, swa_k_norm, swa_sink, swa_w_out, axial_w_in, axial_q_norm, axial_k_norm, axial_w_out):
    prep = _prepare({x_prompt.shape[1], x_sample.shape[1]}, ffn_w13, ffn_w2, norm_w,
                    mlstm_w_in, mlstm_b_gate, mlstm_norm_w, mlstm_w_out,
                    swa_w_in, swa_q_norm, swa_k_norm, swa_sink, swa_w_out,
                    axial_w_in, axial_q_norm, axial_k_norm, axial_w_out)
    nb_prompt = c_prompt.shape[0]
    mod_all = _ada_mod(jnp.concatenate([c_prompt, c_sample], axis=0), ada_w, ada_b)
    y_prompt = _trunk(x_prompt, mod_all[:, :nb_prompt], prep)
    y_sample = _trunk(x_sample, mod_all[:, nb_prompt:], prep)
    return (y_prompt, y_sample)
```

```python
import functools

import jax
import jax.numpy as jnp
from jax import lax
from jax.experimental import pallas as pl
from jax.experimental.pallas import tpu as pltpu

F32 = jnp.float32
BF16 = jnp.bfloat16

D_MODEL = 1024
DEPTH = 4
N_MIXERS = 3
D_FF = 2816
EPS = 1e-6
N_MOD = 9
A_HEADS = 8
A_DK = 64
A_DV = 128
A_CHUNK = 128
A_GATE_CAP = 15.0
ATT_HEADS = 16
ATT_KV_HEADS = 4
ATT_GROUP = ATT_HEADS // ATT_KV_HEADS
HEAD_DIM = 64
WINDOW = 128
BLOCK = 128
ROPE_THETA = 10000.0
GRID_W = 64

V7X_VMEM_BYTES = 64 * 1024 * 1024
V7X_LANES = 128
V7X_MXU_DIM = 256

FFN_CHUNK = V7X_MXU_DIM
N_FFN_CHUNKS = D_FF // FFN_CHUNK
QK_A = A_HEADS * A_DK
V_A = A_HEADS * A_DV
Q_ATT = ATT_HEADS * HEAD_DIM
KV_ATT = ATT_KV_HEADS * HEAD_DIM
VX_ATT = ATT_KV_HEADS * V7X_LANES
LOG2E = 1.4426950408889634
AXIAL_AHEAD = 3
AXIAL_UNROLL = 2
BOUND_SLACK = 1.02
SCORE_BOUND_MAX = 48.0
SWA_STAGE_HEADS = 8

NT_DIMS = (((1,), (1,)), ((), ()))


def _vmem_limit(est_bytes):
    return int(min(est_bytes * 5 // 4 + (4 << 20), V7X_VMEM_BYTES - (4 << 20)))


def _params(sem, est_bytes):
    return pltpu.CompilerParams(dimension_semantics=sem, vmem_limit_bytes=_vmem_limit(est_bytes))


def _resident(shape):
    nd = len(shape)
    return pl.BlockSpec(shape, lambda *_: (0,) * nd, pipeline_mode=pl.Buffered(1))


def _dot(a, b):
    return jnp.dot(a, b, preferred_element_type=F32)


def _dot_nt(a, b):
    return lax.dot_general(a, b, NT_DIMS, preferred_element_type=F32)


def _norm_mod(x, nw, shift, scale):
    ms = jnp.mean(x * x, axis=-1, keepdims=True)
    y = x * lax.rsqrt(ms + EPS)
    return (y * nw) * (1.0 + scale) + shift


def _sigmoid(x):
    return 1.0 / (1.0 + jnp.exp(-x))


def _ada_kernel(c_ref, w_ref, b_ref, o_ref):
    c = c_ref[...]
    a = c * _sigmoid(c)
    o_ref[...] = _dot(a, w_ref[...]) + b_ref[...]


def _ada_mod(c_all, ada_w, ada_b):
    bc = c_all.shape[0]
    n_out = N_MOD * D_MODEL
    tn = D_MODEL
    return pl.pallas_call(
        _ada_kernel,
        grid=(DEPTH, n_out // tn),
        in_specs=[
            pl.BlockSpec((bc, D_MODEL), lambda i, n: (0, 0)),
            pl.BlockSpec((None, D_MODEL, tn), lambda i, n: (i, 0, n)),
            pl.BlockSpec((None, 1, tn), lambda i, n: (i, 0, n)),
        ],
        out_specs=pl.BlockSpec((None, bc, tn), lambda i, n: (i, 0, n)),
        out_shape=jax.ShapeDtypeStruct((DEPTH, bc, n_out), F32),
        compiler_params=_params(("parallel", "parallel"), 4 * (2 * D_MODEL * tn + 4 * bc * tn)),
        name="ada_mod",
    )(c_all, ada_w, ada_b.reshape(DEPTH, 1, n_out))


def _ffn_kernel(*refs, row0, mixer_out):
    if mixer_out:
        x_ref, attn_ref, wout_ref, mod_ref, nw_ref, w1_ref, w3_ref, w2_ref, o_ref, h_ref, a_ref, acc_ref = refs
        o_ref[...] = x_ref[...] + mod_ref[5:6, :] * _dot(attn_ref[...], wout_ref[...])
        x_ref = o_ref
    else:
        x_ref, mod_ref, nw_ref, w1_ref, w3_ref, w2_ref, o_ref, h_ref, a_ref, acc_ref = refs
    shift = mod_ref[row0:row0 + 1, :]
    scale = mod_ref[row0 + 1:row0 + 2, :]
    gate = mod_ref[row0 + 2:row0 + 3, :]
    h_ref[...] = _norm_mod(x_ref[...], nw_ref[...], shift, scale).astype(BF16)

    def act(c):
        h = h_ref[...]
        g = _dot(h, w1_ref[c])
        u = _dot(h, w3_ref[c])
        return ((g * _sigmoid(g)) * u).astype(BF16)

    a_ref[0] = act(0)
    a_ref[1] = act(1)
    acc_ref[...] = _dot(a_ref[0], w2_ref[0])

    def body(c, carry):
        a_next = act(c + 1)
        acc_ref[...] += _dot(a_ref[c % 2], w2_ref[c])
        a_ref[(c + 1) % 2] = a_next
        return carry

    lax.fori_loop(1, N_FFN_CHUNKS - 1, body, 0, unroll=3)
    last = N_FFN_CHUNKS - 1
    o_ref[...] = x_ref[...] + (0.5 * gate) * (acc_ref[...] + _dot(a_ref[last % 2], w2_ref[last]))


def _ffn(x, mod, nw, w1, w3, w2, *, row0, tm, attn=None, wout=None):
    b, s, _ = x.shape
    tile = pl.BlockSpec((None, tm, D_MODEL), lambda i, m: (i, m, 0))
    mixer_out = attn is not None
    est = 4 * tm * D_MODEL * 4 + 3 * D_MODEL * D_FF * 2 + tm * D_MODEL * 6 + 4 * tm * FFN_CHUNK * 4 \
        + 2 * tm * FFN_CHUNK * 2
    mixer_specs, mixer_args = [], []
    if mixer_out:
        est += 2 * tm * D_MODEL * 2 + D_MODEL * D_MODEL * 2
        mixer_specs, mixer_args = [tile, _resident(wout.shape)], [attn, wout]
    return pl.pallas_call(
        functools.partial(_ffn_kernel, row0=row0, mixer_out=mixer_out),
        grid=(b, s // tm),
        in_specs=[tile] + mixer_specs + [
            pl.BlockSpec((None, N_MOD, D_MODEL), lambda i, m: (i, 0, 0)),
            pl.BlockSpec((1, D_MODEL), lambda i, m: (0, 0)),
            _resident(w1.shape),
            _resident(w3.shape),
            _resident(w2.shape),
        ],
        out_specs=tile,
        out_shape=jax.ShapeDtypeStruct(x.shape, F32),
        scratch_shapes=[pltpu.VMEM((tm, D_MODEL), BF16), pltpu.VMEM((2, tm, FFN_CHUNK), BF16),
                        pltpu.VMEM((tm, D_MODEL), F32)],
        compiler_params=_params(("parallel", "parallel"), est),
        name="ffn_mixer_out" if mixer_out else "ffn",
    )(x, *mixer_args, mod, nw, w1, w3, w2)


def _log_sigmoid(x):
    return jnp.minimum(x, 0.0) - jnp.log1p(jnp.exp(-jnp.abs(x)))


def _gate_act(g, is_forget):
    g = A_GATE_CAP * jnp.tanh(g / A_GATE_CAP)
    return jnp.where(is_forget, _log_sigmoid(g), g)


def _mlstm_in_kernel(x_ref, mod_ref, nw_ref, wq_ref, wkt_ref, wv_ref, wgt_ref, bgt_ref,
                     q_ref, kt_ref, v_ref, rows_ref, cols_ref):
    L, H = A_CHUNK, A_HEADS
    cpt = kt_ref.shape[0]
    h = _norm_mod(x_ref[...], nw_ref[...], mod_ref[3:4, :], mod_ref[4:5, :]).astype(BF16)
    g_row = _dot_nt(wgt_ref[...], h) + bgt_ref[...]
    q_ref[...] = (_dot(h, wq_ref[...]) * (A_DK ** -0.5)).astype(BF16)

    row_id = lax.broadcasted_iota(jnp.int32, g_row.shape, 0)
    g_row = _gate_act(g_row, (row_id // H) % 2 == 1)
    ri = lax.broadcasted_iota(jnp.int32, (L, L), 0)
    ci = lax.broadcasted_iota(jnp.int32, (L, L), 1)
    tris = (jnp.where(ri <= ci, 1.0, 0.0).astype(BF16), jnp.where(ri >= ci, 1.0, 0.0).astype(BF16))
    eye = jnp.where(ri == ci, 1.0, 0.0).astype(BF16)
    lane = lax.broadcasted_iota(jnp.int32, (cpt * H, L), 1)

    def chunk_rows(first_row):
        return jnp.concatenate([g_row[first_row:first_row + H, j * L:(j + 1) * L] for j in range(cpt)], axis=0)

    igs = [chunk_rows(2 * d * H) for d in range(2)]
    bs = [_cumsum_rows(chunk_rows((2 * d + 1) * H), tris[d]) for d in range(2)]
    v_ref[...] = _dot(h, wv_ref[...]).astype(BF16)

    col_parts = []
    for d in range(2):
        b, ig = bs[d], igs[d]
        z = b - ig
        cm = _cummax_lanes(-z, lane, reverse=(d == 1))
        b_last = b[:, L - 1:L] if d == 0 else b[:, 0:1]
        log_w = (b_last - b) + ig
        lw_max = jnp.broadcast_to(jnp.max(log_w, axis=1, keepdims=True), b.shape)
        b_last = jnp.broadcast_to(b_last, b.shape)
        for j in range(cpt):
            rows_ref[j, d] = jnp.concatenate([a[j * H:(j + 1) * H, :] for a in (z, log_w, lw_max, b_last)], axis=0)
        col_parts.append((cm, b))
    kt = _dot_nt(wkt_ref[...], h).astype(BF16)
    for j in range(cpt):
        kt_ref[j] = kt[:, j * L:(j + 1) * L]
        packed = jnp.concatenate([a[j * H:(j + 1) * H, :] for d in range(2) for a in col_parts[d]]
                                 + [jnp.zeros((L - 4 * H, L), F32)], axis=0)
        cols_ref[j] = _transpose_rows(eye, packed)


def _mlstm_in(x, mod, nw, wq, wkt, wv, wgt, bgt, *, tm):
    b, s, _ = x.shape
    nch, cpt = s // A_CHUNK, tm // A_CHUNK
    ng = 4 * A_HEADS
    est = 2 * tm * D_MODEL * 4 + 2 * (D_MODEL * (2 * QK_A + V_A + V7X_LANES)) * 2 \
        + 2 * tm * (QK_A * 2 * 2 + V_A * 2 + 2 * V7X_LANES * 4) + tm * (2 * QK_A + V_A) * 4 + (4 << 20)
    return pl.pallas_call(
        _mlstm_in_kernel,
        grid=(b, s // tm),
        in_specs=[
            pl.BlockSpec((None, tm, D_MODEL), lambda i, m: (i, m, 0)),
            pl.BlockSpec((None, N_MOD, D_MODEL), lambda i, m: (i, 0, 0)),
            pl.BlockSpec((1, D_MODEL), lambda i, m: (0, 0)),
            _resident(wq.shape), _resident(wkt.shape), _resident(wv.shape),
            _resident(wgt.shape), _resident(bgt.shape),
        ],
        out_specs=[
            pl.BlockSpec((None, tm, QK_A), lambda i, m: (i, m, 0)),
            pl.BlockSpec((None, cpt, QK_A, A_CHUNK), lambda i, m: (i, m, 0, 0)),
            pl.BlockSpec((None, tm, V_A), lambda i, m: (i, m, 0)),
            pl.BlockSpec((None, cpt, 2, ng, A_CHUNK), lambda i, m: (i, m, 0, 0, 0)),
            pl.BlockSpec((None, cpt, A_CHUNK, A_CHUNK), lambda i, m: (i, m, 0, 0)),
        ],
        out_shape=[
            jax.ShapeDtypeStruct((b, s, QK_A), BF16),
            jax.ShapeDtypeStruct((b, nch, QK_A, A_CHUNK), BF16),
            jax.ShapeDtypeStruct((b, s, V_A), BF16),
            jax.ShapeDtypeStruct((b, nch, 2, ng, A_CHUNK), F32),
            jax.ShapeDtypeStruct((b, nch, A_CHUNK, A_CHUNK), F32),
        ],
        compiler_params=_params(("parallel", "parallel"), est),
        name="mlstm_in",
    )(x, mod, nw, wq, wkt, wv, wgt, bgt)


def _split3(a):
    hi = a.astype(BF16)
    r1 = a - hi.astype(F32)
    mid = r1.astype(BF16)
    lo = (r1 - mid.astype(F32)).astype(BF16)
    return hi, mid, lo


def _cumsum_rows(a, tri):
    hi, mid, lo = _split3(a)
    return _dot(hi, tri) + _dot(mid, tri) + _dot(lo, tri)


def _transpose_rows(eye, a):
    hi, mid, lo = _split3(a)
    return _dot_nt(eye, hi) + _dot_nt(eye, mid) + _dot_nt(eye, lo)


def _cummax_lanes(x, lane, reverse):
    n = x.shape[1]
    k = 1
    while k < n:
        if reverse:
            shifted = jnp.where(lane < n - k, pltpu.roll(x, n - k, axis=1), -jnp.inf)
        else:
            shifted = jnp.where(lane >= k, pltpu.roll(x, k, axis=1), -jnp.inf)
        x = jnp.maximum(x, shifted)
        k *= 2
    return x


def _mlstm_scan_kernel(qf_ref, ktf_ref, vf_ref, rowsf_ref, colsf_ref, qb_ref, ktb_ref, vb_ref, rowsb_ref, colsb_ref,
                       hf_ref, hb_ref, c_ref, m_ref, *, cps):
    L = A_CHUNK
    H = A_HEADS

    @pl.when(pl.program_id(1) == 0)
    def _():
        c_ref[...] = jnp.zeros_like(c_ref)
        m_ref[...] = jnp.full(m_ref.shape, -jnp.inf, F32)

    ri = lax.broadcasted_iota(jnp.int32, (L, L), 0)
    ci = lax.broadcasted_iota(jnp.int32, (L, L), 1)
    keeps = (ri >= ci, ri <= ci)
    ones_blk = jnp.ones((L, L), BF16)
    refs = ((qf_ref, ktf_ref, vf_ref, rowsf_ref, colsf_ref, hf_ref),
            (qb_ref, ktb_ref, vb_ref, rowsb_ref, colsb_ref, hb_ref))

    def body(c, carry):
        stages = []
        for d in range(2):
            q_ref, kt_ref, v_ref, rows_ref, cols_ref, out_ref = refs[d]
            cc = c if d == 0 else cps - 1 - c
            r0 = pl.multiple_of(cc * L, L)
            rows = rows_ref[cc, d]
            z, log_w, lw_max, b_last = (rows[k * H:(k + 1) * H, :] for k in range(4))
            cols = cols_ref[cc]
            m_prev = m_ref[d * H:(d + 1) * H, :]
            m_new = jnp.maximum(b_last + m_prev, lw_max)
            w = jnp.exp(log_w - m_new)
            decay = jnp.exp((b_last + m_prev) - m_new)
            m_ref[d * H:(d + 1) * H, :] = m_new
            q_all = q_ref[pl.ds(r0, L), :]
            kt_all = kt_ref[cc]
            for hd in range(H):
                stages.append(dict(
                    d=d, sd=d * H + hd, out_ref=out_ref, r0=r0, hd=hd, cols=cols,
                    q=q_all[:, hd * A_DK:(hd + 1) * A_DK], kt=kt_all[hd * A_DK:(hd + 1) * A_DK, :],
                    v_ext=jnp.concatenate([v_ref[pl.ds(r0, L), hd * A_DV:(hd + 1) * A_DV], ones_blk], axis=1),
                    z=z[hd:hd + 1, :], m_prev=m_prev[hd:hd + 1, :], w=w[hd:hd + 1, :],
                    decay=decay[hd:hd + 1, 0:1]))
        for st in stages:
            st["c_prev"] = c_ref[st["sd"]]
            st["qk"] = _dot(st["q"], st["kt"])
            st["qc"] = _dot(st["q"], st["c_prev"].astype(BF16))
        for st in stages:
            lane = 2 * st["d"] * H + st["hd"]
            cm_b = jnp.take_along_axis(st["cols"], jnp.full((L, L), lane, jnp.int32), axis=1)
            b_b = jnp.take_along_axis(st["cols"], jnp.full((L, L), lane + H, jnp.int32), axis=1)
            u_b = -jnp.maximum(st["m_prev"], cm_b)
            st["dmat"] = jnp.exp(jnp.where(keeps[st["d"]], u_b - st["z"], -jnp.inf))
            w_inter = jnp.exp(u_b + st["m_prev"])
            st["w_inter"] = jnp.concatenate([w_inter, w_inter], axis=1)
            st["floor"] = jnp.exp(u_b - b_b)
        for st in stages:
            st["sv"] = _dot((st["qk"] * st["dmat"]).astype(BF16), st["v_ext"])
            kw = (st["kt"].astype(F32) * st["w"]).astype(BF16)
            st["upd"] = _dot(kw, st["v_ext"])
        for st in stages:
            r = st["sv"] + st["w_inter"] * st["qc"]
            floor = st["floor"]
            hd = st["hd"]
            st["out_ref"][pl.ds(st["r0"], L), hd * A_DV:(hd + 1) * A_DV] = \
                r[:, :L] / jnp.maximum(jnp.abs(r[:, L:]), floor)
            c_ref[st["sd"]] = st["decay"] * st["c_prev"] + st["upd"]
        return carry

    lax.fori_loop(0, cps, body, 0)


def _mlstm_scan(q, kt, v, rows, cols, *, tb):
    b, s, _ = q.shape
    cps = tb // A_CHUNK
    nb = s // tb
    ng = 4 * A_HEADS

    def specs(idx):
        return [
            pl.BlockSpec((None, tb, QK_A), lambda i, j: (i, idx(j), 0)),
            pl.BlockSpec((None, cps, QK_A, A_CHUNK), lambda i, j: (i, idx(j), 0, 0)),
            pl.BlockSpec((None, tb, V_A), lambda i, j: (i, idx(j), 0)),
            pl.BlockSpec((None, cps, 2, ng, A_CHUNK), lambda i, j: (i, idx(j), 0, 0, 0)),
            pl.BlockSpec((None, cps, A_CHUNK, A_CHUNK), lambda i, j: (i, idx(j), 0, 0)),
        ]

    fwd = lambda j: j
    bwd = lambda j: nb - 1 - j
    est = 2 * 2 * tb * (QK_A * 2 * 2 + V_A * 2 + V7X_LANES * 4) + 2 * 2 * tb * V_A * 4 \
        + 2 * A_HEADS * A_DK * 2 * A_CHUNK * 4 + (16 << 20)
    return pl.pallas_call(
        functools.partial(_mlstm_scan_kernel, cps=cps),
        grid=(b, nb),
        in_specs=specs(fwd) + specs(bwd),
        out_specs=[
            pl.BlockSpec((None, tb, V_A), lambda i, j: (i, j, 0)),
            pl.BlockSpec((None, tb, V_A), lambda i, j: (i, nb - 1 - j, 0)),
        ],
        out_shape=[jax.ShapeDtypeStruct((b, s, V_A), F32)] * 2,
        scratch_shapes=[
            pltpu.VMEM((2 * A_HEADS, A_DK, 2 * A_CHUNK), F32),
            pltpu.VMEM((2 * A_HEADS, A_CHUNK), F32),
        ],
        compiler_params=_params(("parallel", "arbitrary"), est),
        name="mlstm_scan",
    )(q, kt, v, rows, cols, q, kt, v, rows, cols)


def _mlstm_out_kernel(x_ref, hf_ref, hb_ref, mod_ref, nw_ref, wo_ref, mnw_ref, wout_ref, o_ref):
    x = x_ref[...]
    h = _norm_mod(x, nw_ref[...], mod_ref[3:4, :], mod_ref[4:5, :]).astype(BF16)
    o_gate = _dot(h, wo_ref[...])
    hs = hf_ref[...] + hb_ref[...]
    parts = []
    for hd in range(A_HEADS):
        a = hs[:, hd * A_DV:(hd + 1) * A_DV]
        ms = jnp.mean(a * a, axis=-1, keepdims=True)
        parts.append(a * lax.rsqrt(ms + EPS))
    y = jnp.concatenate(parts, axis=1) * mnw_ref[...]
    z = (y * _sigmoid(o_gate)).astype(BF16)
    o_ref[...] = x + mod_ref[5:6, :] * _dot(z, wout_ref[...])


def _mlstm_out(x, hf, hb, mod, nw, wo, mnw, wout, *, tm):
    b, s, _ = x.shape
    tile = pl.BlockSpec((None, tm, D_MODEL), lambda i, m: (i, m, 0))
    est = 2 * 4 * tm * D_MODEL * 4 + 2 * D_MODEL * D_MODEL * 2 + 6 * tm * D_MODEL * 4
    return pl.pallas_call(
        _mlstm_out_kernel,
        grid=(b, s // tm),
        in_specs=[
            tile, tile, tile,
            pl.BlockSpec((None, N_MOD, D_MODEL), lambda i, m: (i, 0, 0)),
            pl.BlockSpec((1, D_MODEL), lambda i, m: (0, 0)),
            _resident(wo.shape),
            pl.BlockSpec((1, V_A), lambda i, m: (0, 0)),
            _resident(wout.shape),
        ],
        out_specs=tile,
        out_shape=jax.ShapeDtypeStruct(x.shape, F32),
        compiler_params=_params(("parallel", "parallel"), est),
        name="mlstm_out",
    )(x, hf, hb, mod, nw, wo, mnw, wout)


def _attn_in_kernel(x_ref, mod_ref, nw_ref, wq_ref, wqr_ref, wkt_ref, wkrt_ref, wv_ref, seg_ref,
                    cq_ref, sq_ref, ckt_ref, skt_ref, qw_ref, qwsw_ref, kw_ref, kwsw_ref,
                    q_ref, kt_ref, v_ref, *, q_scale):
    h = _norm_mod(x_ref[...], nw_ref[...], mod_ref[3:4, :], mod_ref[4:5, :]).astype(BF16)
    v = _dot(h, wv_ref[...])
    lane = lax.broadcasted_iota(jnp.int32, v.shape, 1)
    v_ref[...] = jnp.where(lane % V7X_LANES == HEAD_DIM, 1.0, v).astype(BF16)

    q = _dot(h, wq_ref[...])
    q_rot = _dot(h, wqr_ref[...])
    ssq = _dot((q * q).astype(BF16), seg_ref[...])
    rq = lax.rsqrt(ssq * (1.0 / HEAD_DIM) + EPS)
    reps = Q_ATT // V7X_LANES
    cos_q = jnp.concatenate([cq_ref[...] * qw_ref[...]] * reps, axis=1)
    sin_q = jnp.concatenate([sq_ref[...] * qwsw_ref[...]] * reps, axis=1)
    q_ref[...] = ((rq * q_scale) * (q * cos_q + q_rot * sin_q)).astype(BF16)

    kt = _dot_nt(wkt_ref[...], h)
    kt_rot = _dot_nt(wkrt_ref[...], h)
    cos_k = ckt_ref[...] * kw_ref[...]
    sin_k = skt_ref[...] * kwsw_ref[...]
    outs = []
    for j in range(ATT_KV_HEADS):
        a = kt[j * HEAD_DIM:(j + 1) * HEAD_DIM, :]
        ar = kt_rot[j * HEAD_DIM:(j + 1) * HEAD_DIM, :]
        rk = lax.rsqrt(jnp.mean(a * a, axis=0, keepdims=True) + EPS)
        outs.append(rk * (a * cos_k + ar * sin_k))
    kt_ref[...] = jnp.concatenate(outs, axis=0).astype(BF16)


def _attn_in(x, mod, nw, w, tabs, *, tm, tk, q_scale):
    b, s, _ = x.shape
    per = tk // tm
    const2 = lambda i, m: (0, 0)
    est = 2 * tm * D_MODEL * 4 + (3 * D_MODEL * Q_ATT + 3 * D_MODEL * KV_ATT) * 2 \
        + 2 * tm * (Q_ATT + 2 * KV_ATT) * 2 + 8 * tm * Q_ATT * 4 + 8 * tm * V7X_LANES * 4
    return pl.pallas_call(
        functools.partial(_attn_in_kernel, q_scale=q_scale),
        grid=(b, s // tm),
        in_specs=[
            pl.BlockSpec((None, tm, D_MODEL), lambda i, m: (i, m, 0)),
            pl.BlockSpec((None, N_MOD, D_MODEL), lambda i, m: (i, 0, 0)),
            pl.BlockSpec((1, D_MODEL), const2),
            _resident(w["wq"].shape), _resident(w["wqr"].shape),
            _resident(w["wkt"].shape), _resident(w["wkrt"].shape),
            _resident(w["wv"].shape), _resident(w["seg"].shape),
            pl.BlockSpec((tm, V7X_LANES), lambda i, m: (m, 0)),
            pl.BlockSpec((tm, V7X_LANES), lambda i, m: (m, 0)),
            pl.BlockSpec((HEAD_DIM, tm), lambda i, m: (0, m)),
            pl.BlockSpec((HEAD_DIM, tm), lambda i, m: (0, m)),
            pl.BlockSpec((1, V7X_LANES), const2), pl.BlockSpec((1, V7X_LANES), const2),
            pl.BlockSpec((HEAD_DIM, 1), const2), pl.BlockSpec((HEAD_DIM, 1), const2),
        ],
        out_specs=[
            pl.BlockSpec((None, tm, Q_ATT), lambda i, m: (i, m, 0)),
            pl.BlockSpec((None, None, KV_ATT, tm), lambda i, m: (i, m // per, 0, m % per)),
            pl.BlockSpec((None, tm, VX_ATT), lambda i, m: (i, m, 0)),
        ],
        out_shape=[
            jax.ShapeDtypeStruct((b, s, Q_ATT), BF16),
            jax.ShapeDtypeStruct((b, s // tk, KV_ATT, tk), BF16),
            jax.ShapeDtypeStruct((b, s, VX_ATT), BF16),
        ],
        compiler_params=_params(("parallel", "parallel"), est),
        name="attn_in",
    )(x, mod, nw, w["wq"], w["wqr"], w["wkt"], w["wkrt"], w["wv"], w["seg"],
      tabs["cos_q"], tabs["sin_q"], tabs["cos_kt"], tabs["sin_kt"],
      w["qw"], w["qwsw"], w["kw"], w["kwsw"])


def _swa_kernel(bound_ref, sink_ref, q_ref, ktl_ref, ktm_ref, ktr_ref, vl_ref, vm_ref, vr_ref, o_ref,
                *, seq, tq, bounded):
    nsub = tq // BLOCK
    j = pl.program_id(1)
    kt_win = jnp.concatenate([ktl_ref[...], ktm_ref[...], ktr_ref[...]], axis=1)
    v_win = jnp.concatenate([vl_ref[...], vm_ref[...], vr_ref[...]], axis=0)
    qi = lax.broadcasted_iota(jnp.int32, (BLOCK, 3 * BLOCK), 0)
    kj = lax.broadcasted_iota(jnp.int32, (BLOCK, 3 * BLOCK), 1) - BLOCK
    band_bias = jnp.where(jnp.abs(qi - kj) <= WINDOW, 0.0, -jnp.inf)
    biases = []
    for i in range(nsub):
        bias = band_bias
        if i == 0:
            bias = jnp.where(kj + j * tq >= 0, bias, -jnp.inf)
        if i == nsub - 1:
            bias = jnp.where(kj + (j * tq + i * BLOCK) < seq, bias, -jnp.inf)
        biases.append(bias)
    q_tiles = [q_ref[i * BLOCK:(i + 1) * BLOCK, :] for i in range(nsub)]

    per_blk = ATT_HEADS // SWA_STAGE_HEADS
    n_stage = nsub * per_blk

    def heads_of(n):
        i, part = divmod(n, per_blk)
        return i, range(part * SWA_STAGE_HEADS, (part + 1) * SWA_STAGE_HEADS)

    def scores(n):
        i, heads = heads_of(n)
        return [_dot(q_tiles[i][:, h * HEAD_DIM:(h + 1) * HEAD_DIM],
                     kt_win[(h // ATT_GROUP) * HEAD_DIM:(h // ATT_GROUP + 1) * HEAD_DIM, i * BLOCK:(i + 3) * BLOCK])
                + biases[i] for h in heads]

    s_next = scores(0)
    outs = []
    for n in range(n_stage):
        s_heads = s_next
        if n + 1 < n_stage:
            s_next = scores(n + 1)
        i, heads = heads_of(n)
        sinks = [sink_ref[h] * LOG2E for h in heads]
        if bounded:
            shifts = [jnp.maximum(bound_ref[0], sk) for sk in sinks]
            p = jnp.concatenate([jnp.exp2(s - m) for s, m in zip(s_heads, shifts)], axis=0).astype(BF16)
            sink_w = jnp.concatenate([jnp.full((BLOCK, 1), jnp.exp2(sk - m), F32) for sk, m in zip(sinks, shifts)],
                                     axis=0)
        else:
            s = jnp.concatenate(s_heads, axis=0)
            sink = jnp.concatenate([jnp.full((BLOCK, 1), sk, F32) for sk in sinks], axis=0)
            m = jnp.maximum(jnp.max(s, axis=1, keepdims=True), sink)
            p = jnp.exp2(s - m).astype(BF16)
            sink_w = jnp.exp2(sink - m)
        r = jnp.concatenate(
            [_dot(p[a * BLOCK:(a + 1) * BLOCK, :],
                  v_win[i * BLOCK:(i + 3) * BLOCK, (h // ATT_GROUP) * V7X_LANES:(h // ATT_GROUP + 1) * V7X_LANES])
             for a, h in enumerate(heads)], axis=0)
        o = r[:, :HEAD_DIM] / (r[:, HEAD_DIM:HEAD_DIM + 1] + sink_w)
        outs += [o[a * BLOCK:(a + 1) * BLOCK, :] for a in range(SWA_STAGE_HEADS)]
        if len(outs) == ATT_HEADS:
            o_ref[i * BLOCK:(i + 1) * BLOCK, :] = jnp.concatenate(outs, axis=1).astype(BF16)
            outs = []


def _swa(bound, q, kt, v, sink, *, tq, bounded):
    b, s, _ = q.shape
    nsub = tq // BLOCK
    nblk = s // BLOCK
    nq = s // tq
    left = lambda j: jnp.maximum(j * nsub - 1, 0)
    right = lambda j: jnp.minimum((j + 1) * nsub, nblk - 1)
    est = 2 * (tq * Q_ATT * 2 * 2 + (tq + 2 * BLOCK) * (KV_ATT + VX_ATT) * 2) + (12 << 20)
    return pl.pallas_call(
        functools.partial(_swa_kernel, seq=s, tq=tq, bounded=bounded),
        grid=(b, nq),
        in_specs=[
            pl.BlockSpec(memory_space=pltpu.SMEM),
            pl.BlockSpec(memory_space=pltpu.SMEM),
            pl.BlockSpec((None, tq, Q_ATT), lambda i, j: (i, j, 0)),
            pl.BlockSpec((None, None, KV_ATT, BLOCK), lambda i, j: (i, 0, 0, left(j))),
            pl.BlockSpec((None, None, KV_ATT, tq), lambda i, j: (i, 0, 0, j)),
            pl.BlockSpec((None, None, KV_ATT, BLOCK), lambda i, j: (i, 0, 0, right(j))),
            pl.BlockSpec((None, BLOCK, VX_ATT), lambda i, j: (i, left(j), 0)),
            pl.BlockSpec((None, tq, VX_ATT), lambda i, j: (i, j, 0)),
            pl.BlockSpec((None, BLOCK, VX_ATT), lambda i, j: (i, right(j), 0)),
        ],
        out_specs=pl.BlockSpec((None, tq, Q_ATT), lambda i, j: (i, j, 0)),
        out_shape=jax.ShapeDtypeStruct((b, s, Q_ATT), BF16),
        compiler_params=_params(("parallel", "parallel"), est),
        name="swa_bounded" if bounded else "swa",
    )(bound, sink, q, kt, kt, kt, v, v, v)


def _axial_kernel(bound_ref, q_ref, kt_ref, v_ref, o_ref, m_ref, acc_ref, s_ref, *, tk, nk, bounded):
    q_rows = q_ref[...]
    qs = [q_rows[:, h * HEAD_DIM:(h + 1) * HEAD_DIM] for h in range(ATT_HEADS)]
    m_ref[...] = jnp.full(m_ref.shape, -jnp.inf, F32)
    acc_ref[...] = jnp.zeros_like(acc_ref)

    def scores(h, kt_c):
        g = h // ATT_GROUP
        return _dot(qs[h], kt_c[g * HEAD_DIM:(g + 1) * HEAD_DIM, :])

    kt_0 = kt_ref[0]
    for h in range(AXIAL_AHEAD):
        s_ref[h] = scores(h, kt_0)

    def body(c, carry):
        kt_c = kt_ref[c]
        kt_n = kt_ref[jnp.minimum(c + 1, nk - 1)]
        v_c = v_ref[pl.ds(pl.multiple_of(c * tk, tk), tk), :]
        pending = {h: s_ref[h] for h in range(AXIAL_AHEAD)}
        for h in range(ATT_HEADS):
            ahead = h + AXIAL_AHEAD
            if ahead < ATT_HEADS:
                pending[ahead] = scores(ahead, kt_c)
            else:
                s_ref[ahead - ATT_HEADS] = scores(ahead - ATT_HEADS, kt_n)
            g = h // ATT_GROUP
            s = pending.pop(h)
            v_ext = v_c[:, g * V7X_LANES:(g + 1) * V7X_LANES]
            if bounded:
                acc_ref[h] += _dot(jnp.exp2(s - bound_ref[0]).astype(BF16), v_ext)
            else:
                m_prev = m_ref[h]
                m_new = jnp.maximum(m_prev, jnp.max(s, axis=1, keepdims=True))
                p = jnp.exp2(s - m_new).astype(BF16)
                acc_ref[h] = jnp.exp2(m_prev - m_new) * acc_ref[h] + _dot(p, v_ext)
                m_ref[h] = m_new
        return carry

    lax.fori_loop(0, nk, body, 0, unroll=min(nk, AXIAL_UNROLL))
    outs = []
    for h in range(ATT_HEADS):
        acc = acc_ref[h]
        outs.append(acc[:, :HEAD_DIM] / acc[:, HEAD_DIM:HEAD_DIM + 1])
    o_ref[...] = jnp.concatenate(outs, axis=1).astype(BF16)


def _axial(bound, q, kt, v, *, tq, tk, bounded):
    b, s, _ = q.shape
    nk = s // tk
    rows = ATT_GROUP * tq
    est = 2 * tq * Q_ATT * 2 * 2 + s * (KV_ATT + VX_ATT) * 2 \
        + ATT_KV_HEADS * (2 * rows * V7X_LANES * 4 + rows * tk * 6) + (4 << 20)
    return pl.pallas_call(
        functools.partial(_axial_kernel, tk=tk, nk=nk, bounded=bounded),
        grid=(b, s // tq),
        in_specs=[
            pl.BlockSpec(memory_space=pltpu.SMEM),
            pl.BlockSpec((None, tq, Q_ATT), lambda i, j: (i, j, 0)),
            pl.BlockSpec((None, nk, KV_ATT, tk), lambda i, j: (i, 0, 0, 0), pipeline_mode=pl.Buffered(1)),
            pl.BlockSpec((None, s, VX_ATT), lambda i, j: (i, 0, 0), pipeline_mode=pl.Buffered(1)),
        ],
        out_specs=pl.BlockSpec((None, tq, Q_ATT), lambda i, j: (i, j, 0)),
        out_shape=jax.ShapeDtypeStruct((b, s, Q_ATT), BF16),
        scratch_shapes=[
            pltpu.VMEM((ATT_HEADS, tq, 1), F32), pltpu.VMEM((ATT_HEADS, tq, V7X_LANES), F32),
            pltpu.VMEM((AXIAL_AHEAD, tq, tk), F32),
        ],
        compiler_params=_params(("parallel", "parallel"), est),
        name="axial_bounded" if bounded else "axial",
    )(bound, q, kt, v)


def _rope_tables(pos, dim):
    inv = ROPE_THETA ** (-jnp.arange(0, dim, 2, dtype=F32) / dim)
    ang = pos.astype(F32)[:, None] * inv[None, :]
    ang = jnp.concatenate([ang, ang], axis=-1)
    return jnp.cos(ang), jnp.sin(ang)


def _rot_half_perm(widths):
    perm, sign, base = [], [], 0
    for w in widths:
        half = w // 2
        perm += [base + half + i for i in range(half)] + [base + i for i in range(half)]
        sign += [-1.0] * half + [1.0] * half
        base += w
    return jnp.array(perm, jnp.int32), jnp.array(sign, F32)


def _attn_tables(cos, sin):
    reps = V7X_LANES // HEAD_DIM
    return {
        "cos_q": jnp.tile(cos, (1, reps)), "sin_q": jnp.tile(sin, (1, reps)),
        "cos_kt": cos.T, "sin_kt": sin.T,
    }


def _attn_weights(w_in, q_norm, k_norm, widths):
    perm, sign = _rot_half_perm(widths)
    wq = w_in[:, :Q_ATT]
    wk = w_in[:, Q_ATT:Q_ATT + KV_ATT]
    wv = w_in[:, Q_ATT + KV_ATT:]

    def rotated(w, heads):
        w3 = w.reshape(D_MODEL, heads, HEAD_DIM)
        return (w3[:, :, perm] * sign).reshape(D_MODEL, heads * HEAD_DIM)

    seg_id = jnp.arange(Q_ATT) // HEAD_DIM
    reps = V7X_LANES // HEAD_DIM
    return {
        "wq": wq.astype(BF16), "wqr": rotated(wq, ATT_HEADS).astype(BF16),
        "wkt": wk.T.astype(BF16), "wkrt": rotated(wk, ATT_KV_HEADS).T.astype(BF16),
        "wv": jnp.pad(wv.reshape(D_MODEL, ATT_KV_HEADS, HEAD_DIM),
                      ((0, 0), (0, 0), (0, V7X_LANES - HEAD_DIM))).reshape(D_MODEL, VX_ATT).astype(BF16),
        "seg": (seg_id[:, None] == seg_id[None, :]).astype(BF16),
        "qw": jnp.tile(q_norm, reps)[None, :], "qwsw": jnp.tile(q_norm[perm], reps)[None, :],
        "kw": k_norm[:, None], "kwsw": k_norm[perm][:, None],
        "bound": (BOUND_SLACK * LOG2E * HEAD_DIM ** 0.5
                  * jnp.max(jnp.abs(q_norm)) * jnp.max(jnp.abs(k_norm))).reshape(1).astype(F32),
    }


def _ffn_weights(w13, w2):
    def chunked(w):
        return w.reshape(D_MODEL, N_FFN_CHUNKS, FFN_CHUNK).transpose(1, 0, 2).astype(BF16)
    return chunked(w13[:, :D_FF]), chunked(w13[:, D_FF:]), w2.reshape(N_FFN_CHUNKS, FFN_CHUNK, D_MODEL).astype(BF16)


def _mlstm_weights(w_in, b_gate, norm_w, w_out):
    wq = w_in[:, :QK_A]
    wk = w_in[:, QK_A:2 * QK_A]
    wv = w_in[:, 2 * QK_A:2 * QK_A + V_A]
    wo = w_in[:, 2 * QK_A + V_A:2 * QK_A + 2 * V_A]
    wg = w_in[:, 2 * QK_A + 2 * V_A:]
    return {
        "wq": wq.astype(BF16), "wkt": wk.T.astype(BF16), "wv": wv.astype(BF16), "wo": wo.astype(BF16),
        "wgt": wg.T.astype(BF16), "bgt": b_gate[:, None],
        "mnw": norm_w[None, :], "wout": w_out.astype(BF16),
    }


def _tile(s, want):
    t = min(s, want)
    assert s % t == 0
    return t


def _bounded_or_exact(attend, bound, *operands):
    return lax.cond(bound[0] <= SCORE_BOUND_MAX,
                    functools.partial(attend, bounded=True), functools.partial(attend, bounded=False),
                    bound, *operands)


def _trunk(x, mod_all, prep):
    b, s, _ = x.shape
    tm_ffn = _tile(s, 1024)
    tm = _tile(s, 512)
    for i in range(DEPTH):
        mod = mod_all[i].reshape(b, N_MOD, D_MODEL)
        lw = prep["layers"][i]
        x = _ffn(x, mod, lw["nw"][0:1], *lw["ffn0"], row0=0, tm=tm_ffn)
        kind = i % N_MIXERS
        mw = lw["mixer"]
        if kind == 0:
            q, kt, v, rows, cols = _mlstm_in(x, mod, lw["nw"][1:2], mw["wq"], mw["wkt"], mw["wv"],
                                             mw["wgt"], mw["bgt"], tm=tm)
            hf, hb = _mlstm_scan(q, kt, v, rows, cols, tb=_tile(s, 512))
            x = _mlstm_out(x, hf, hb, mod, lw["nw"][1:2], mw["wo"], mw["mnw"], mw["wout"], tm=tm)
        elif kind == 1:
            q, kt, v = _attn_in(x, mod, lw["nw"][1:2], mw, prep["swa_tabs"][s], tm=tm, tk=s,
                                q_scale=HEAD_DIM ** -0.5 * LOG2E)
            attn = _bounded_or_exact(functools.partial(_swa, tq=_tile(s, 512)), mw["bound"], q, kt, v, mw["sink"])
        else:
            tk = _tile(s, 1024)
            q, kt, v = _attn_in(x, mod, lw["nw"][1:2], mw, prep["axial_tabs"][s], tm=tm, tk=tk,
                                q_scale=HEAD_DIM ** -0.5 * LOG2E)
            attn = _bounded_or_exact(functools.partial(_axial, tq=_tile(s, 128), tk=tk), mw["bound"], q, kt, v)
        if kind == 0:
            x = _ffn(x, mod, lw["nw"][2:3], *lw["ffn1"], row0=6, tm=tm_ffn)
        else:
            x = _ffn(x, mod, lw["nw"][2:3], *lw["ffn1"], row0=6, tm=tm_ffn, attn=attn, wout=mw["wout"])
    return x


def _prepare(seqs, ffn_w13, ffn_w2, norm_w,
             mlstm_w_in, mlstm_b_gate, mlstm_norm_w, mlstm_w_out,
             swa_w_in, swa_q_norm, swa_k_norm, swa_sink, swa_w_out,
             axial_w_in, axial_q_norm, axial_k_norm, axial_w_out):
    layers = []
    for i in range(DEPTH):
        kind, j = i % N_MIXERS, i // N_MIXERS
        if kind == 0:
            mixer = _mlstm_weights(mlstm_w_in[j], mlstm_b_gate[j], mlstm_norm_w[j], mlstm_w_out[j])
        elif kind == 1:
            mixer = _attn_weights(swa_w_in[j], swa_q_norm[j], swa_k_norm[j], (HEAD_DIM,))
            mixer["sink"] = swa_sink[j]
            mixer["wout"] = swa_w_out[j].astype(BF16)
        else:
            mixer = _attn_weights(axial_w_in[j], axial_q_norm[j], axial_k_norm[j], (HEAD_DIM // 2, HEAD_DIM // 2))
            mixer["wout"] = axial_w_out[j].astype(BF16)
        layers.append({
            "nw": norm_w[i],
            "ffn0": _ffn_weights(ffn_w13[i, 0], ffn_w2[i, 0]),
            "ffn1": _ffn_weights(ffn_w13[i, 1], ffn_w2[i, 1]),
            "mixer": mixer,
        })
    swa_tabs, axial_tabs = {}, {}
    for s in seqs:
        swa_tabs[s] = _attn_tables(*_rope_tables(jnp.arange(s), HEAD_DIM))
        rows = s // GRID_W
        row_ids = jnp.repeat(jnp.arange(rows), GRID_W)
        col_ids = jnp.tile(jnp.arange(GRID_W), rows)
        rc, rs = _rope_tables(row_ids, HEAD_DIM // 2)
        cc, cs = _rope_tables(col_ids, HEAD_DIM // 2)
        axial_tabs[s] = _attn_tables(jnp.concatenate([rc, cc], axis=-1), jnp.concatenate([rs, cs], axis=-1))
    return {"layers": layers, "swa_tabs": swa_tabs, "axial_tabs": axial_tabs}


def kernel(x_prompt, x_sample, c_prompt, c_sample, ffn_w13, ffn_w2, ada_w, ada_b, norm_w, mlstm_w_in, mlstm_b_gate, mlstm_norm_w, mlstm_w_out, swa_w_in, swa_q_norm, swa_k_norm, swa_sink, swa_w_out, axial_w_in, axial_q_norm, axial_k_norm, axial_w_out):
    prep = _prepare({x_prompt.shape[1], x_sample.shape[1]}, ffn_w13, ffn_w2, norm_w,
                    mlstm_w_in, mlstm_b_gate, mlstm_norm_w, mlstm_w_out,
                    swa_w_in, swa_q_norm, swa_k_norm, swa_sink, swa_w_out,
                    axial_w_in, axial_q_norm, axial_k_norm, axial_w_out)
    nb_prompt = c_prompt.shape[0]
    mod_all = _ada_mod(jnp.concatenate([c_prompt, c_sample], axis=0), ada_w, ada_b)
    y_prompt = _trunk(x_prompt, mod_all[:, :nb_prompt], prep)
    y_sample = _trunk(x_sample, mod_all[:, nb_prompt:], prep)
    return (y_prompt, y_sample)
```

```python
import functools

import jax
import jax.numpy as jnp
from jax import lax
from jax.experimental import pallas as pl
from jax.experimental.pallas import tpu as pltpu

F32 = jnp.float32
BF16 = jnp.bfloat16

D_MODEL = 1024
DEPTH = 4
N_MIXERS = 3
D_FF = 2816
EPS = 1e-6
N_MOD = 9
A_HEADS = 8
A_DK = 64
A_DV = 128
A_CHUNK = 128
A_GATE_CAP = 15.0
ATT_HEADS = 16
ATT_KV_HEADS = 4
ATT_GROUP = ATT_HEADS // ATT_KV_HEADS
HEAD_DIM = 64
WINDOW = 128
BLOCK = 128
ROPE_THETA = 10000.0
GRID_W = 64

V7X_VMEM_BYTES = 64 * 1024 * 1024
V7X_LANES = 128
V7X_MXU_DIM = 256

FFN_CHUNK = V7X_MXU_DIM
N_FFN_CHUNKS = D_FF // FFN_CHUNK
QK_A = A_HEADS * A_DK
V_A = A_HEADS * A_DV
Q_ATT = ATT_HEADS * HEAD_DIM
KV_ATT = ATT_KV_HEADS * HEAD_DIM
VX_ATT = ATT_KV_HEADS * V7X_LANES
LOG2E = 1.4426950408889634
AXIAL_AHEAD = 3
AXIAL_UNROLL = 2
BOUND_SLACK = 1.02
SCORE_BOUND_MAX = 48.0
SWA_STAGE_HEADS = 8
MLSTM_OUT_SLABS = 2
ATTN_IN_SLABS = 1

NT_DIMS = (((1,), (1,)), ((), ()))


def _vmem_limit(est_bytes):
    return int(min(est_bytes * 5 // 4 + (4 << 20), V7X_VMEM_BYTES - (4 << 20)))


def _params(sem, est_bytes):
    return pltpu.CompilerParams(dimension_semantics=sem, vmem_limit_bytes=_vmem_limit(est_bytes))


def _resident(shape):
    nd = len(shape)
    return pl.BlockSpec(shape, lambda *_: (0,) * nd, pipeline_mode=pl.Buffered(1))


def _dot(a, b):
    return jnp.dot(a, b, preferred_element_type=F32)


def _dot_nt(a, b):
    return lax.dot_general(a, b, NT_DIMS, preferred_element_type=F32)


def _norm_mod(x, nw, shift, scale):
    ms = jnp.mean(x * x, axis=-1, keepdims=True)
    y = x * lax.rsqrt(ms + EPS)
    return (y * nw) * (1.0 + scale) + shift


def _sigmoid(x):
    return 1.0 / (1.0 + jnp.exp(-x))


def _ada_kernel(c_ref, w_ref, b_ref, o_ref):
    c = c_ref[...]
    a = c * _sigmoid(c)
    o_ref[...] = _dot(a, w_ref[...]) + b_ref[...]


def _ada_mod(c_all, ada_w, ada_b):
    bc = c_all.shape[0]
    n_out = N_MOD * D_MODEL
    tn = D_MODEL
    return pl.pallas_call(
        _ada_kernel,
        grid=(DEPTH, n_out // tn),
        in_specs=[
            pl.BlockSpec((bc, D_MODEL), lambda i, n: (0, 0)),
            pl.BlockSpec((None, D_MODEL, tn), lambda i, n: (i, 0, n)),
            pl.BlockSpec((None, 1, tn), lambda i, n: (i, 0, n)),
        ],
        out_specs=pl.BlockSpec((None, bc, tn), lambda i, n: (i, 0, n)),
        out_shape=jax.ShapeDtypeStruct((DEPTH, bc, n_out), F32),
        compiler_params=_params(("parallel", "parallel"), 4 * (2 * D_MODEL * tn + 4 * bc * tn)),
        name="ada_mod",
    )(c_all, ada_w, ada_b.reshape(DEPTH, 1, n_out))


def _ffn_kernel(*refs, row0, mixer_out):
    if mixer_out:
        x_ref, attn_ref, wout_ref, mod_ref, nw_ref, w1_ref, w3_ref, w2_ref, o_ref, h_ref, a_ref, acc_ref = refs
        o_ref[...] = x_ref[...] + mod_ref[5:6, :] * _dot(attn_ref[...], wout_ref[...])
        x_ref = o_ref
    else:
        x_ref, mod_ref, nw_ref, w1_ref, w3_ref, w2_ref, o_ref, h_ref, a_ref, acc_ref = refs
    shift = mod_ref[row0:row0 + 1, :]
    scale = mod_ref[row0 + 1:row0 + 2, :]
    gate = mod_ref[row0 + 2:row0 + 3, :]
    h_ref[...] = _norm_mod(x_ref[...], nw_ref[...], shift, scale).astype(BF16)

    def act(c):
        h = h_ref[...]
        g = _dot(h, w1_ref[c])
        u = _dot(h, w3_ref[c])
        return ((g * _sigmoid(g)) * u).astype(BF16)

    a_ref[0] = act(0)
    a_ref[1] = act(1)
    acc_ref[...] = _dot(a_ref[0], w2_ref[0])

    def body(c, carry):
        a_next = act(c + 1)
        acc_ref[...] += _dot(a_ref[c % 2], w2_ref[c])
        a_ref[(c + 1) % 2] = a_next
        return carry

    lax.fori_loop(1, N_FFN_CHUNKS - 1, body, 0, unroll=True)
    last = N_FFN_CHUNKS - 1
    o_ref[...] = x_ref[...] + (0.5 * gate) * (acc_ref[...] + _dot(a_ref[last % 2], w2_ref[last]))


def _ffn(x, mod, nw, w1, w3, w2, *, row0, tm, attn=None, wout=None):
    b, s, _ = x.shape
    tile = pl.BlockSpec((None, tm, D_MODEL), lambda i, m: (i, m, 0))
    mixer_out = attn is not None
    est = 4 * tm * D_MODEL * 4 + 3 * D_MODEL * D_FF * 2 + tm * D_MODEL * 6 + 4 * tm * FFN_CHUNK * 4 \
        + 2 * tm * FFN_CHUNK * 2
    mixer_specs, mixer_args = [], []
    if mixer_out:
        est += 2 * tm * D_MODEL * 2 + D_MODEL * D_MODEL * 2
        mixer_specs, mixer_args = [tile, _resident(wout.shape)], [attn, wout]
    return pl.pallas_call(
        functools.partial(_ffn_kernel, row0=row0, mixer_out=mixer_out),
        grid=(b, s // tm),
        in_specs=[tile] + mixer_specs + [
            pl.BlockSpec((None, N_MOD, D_MODEL), lambda i, m: (i, 0, 0)),
            pl.BlockSpec((1, D_MODEL), lambda i, m: (0, 0)),
            _resident(w1.shape),
            _resident(w3.shape),
            _resident(w2.shape),
        ],
        out_specs=tile,
        out_shape=jax.ShapeDtypeStruct(x.shape, F32),
        scratch_shapes=[pltpu.VMEM((tm, D_MODEL), BF16), pltpu.VMEM((2, tm, FFN_CHUNK), BF16),
                        pltpu.VMEM((tm, D_MODEL), F32)],
        compiler_params=_params(("parallel", "parallel"), est),
        name="ffn_mixer_out" if mixer_out else "ffn",
    )(x, *mixer_args, mod, nw, w1, w3, w2)


def _log_sigmoid(x):
    return jnp.minimum(x, 0.0) - jnp.log1p(jnp.exp(-jnp.abs(x)))


def _gate_act(g, is_forget):
    g = A_GATE_CAP * jnp.tanh(g / A_GATE_CAP)
    return jnp.where(is_forget, _log_sigmoid(g), g)


def _mlstm_in_kernel(x_ref, mod_ref, nw_ref, wq_ref, wkt_ref, wv_ref, wgt_ref, bgt_ref,
                     q_ref, kt_ref, v_ref, rows_ref, cols_ref):
    L, H = A_CHUNK, A_HEADS
    cpt = kt_ref.shape[0]
    h = _norm_mod(x_ref[...], nw_ref[...], mod_ref[3:4, :], mod_ref[4:5, :]).astype(BF16)
    g_row = _dot_nt(wgt_ref[...], h) + bgt_ref[...]
    q_ref[...] = (_dot(h, wq_ref[...]) * (A_DK ** -0.5)).astype(BF16)

    row_id = lax.broadcasted_iota(jnp.int32, g_row.shape, 0)
    g_row = _gate_act(g_row, (row_id // H) % 2 == 1) * LOG2E
    ri = lax.broadcasted_iota(jnp.int32, (L, L), 0)
    ci = lax.broadcasted_iota(jnp.int32, (L, L), 1)
    tris = (jnp.where(ri <= ci, 1.0, 0.0).astype(BF16), jnp.where(ri >= ci, 1.0, 0.0).astype(BF16))
    eye = jnp.where(ri == ci, 1.0, 0.0).astype(BF16)
    lane = lax.broadcasted_iota(jnp.int32, (cpt * H, L), 1)

    def chunk_rows(first_row):
        return jnp.concatenate([g_row[first_row:first_row + H, j * L:(j + 1) * L] for j in range(cpt)], axis=0)

    igs = [chunk_rows(2 * d * H) for d in range(2)]
    bs = [_cumsum_rows(chunk_rows((2 * d + 1) * H), tris[d]) for d in range(2)]
    v_ref[...] = _dot(h, wv_ref[...]).astype(BF16)

    col_parts = []
    for d in range(2):
        b, ig = bs[d], igs[d]
        z = b - ig
        cm = _cummax_lanes(-z, lane, reverse=(d == 1))
        b_last = b[:, L - 1:L] if d == 0 else b[:, 0:1]
        log_w = (b_last - b) + ig
        lw_max = jnp.broadcast_to(jnp.max(log_w, axis=1, keepdims=True), b.shape)
        b_last = jnp.broadcast_to(b_last, b.shape)
        for j in range(cpt):
            rows_ref[j, d] = jnp.concatenate([a[j * H:(j + 1) * H, :] for a in (z, log_w, lw_max, b_last)], axis=0)
        col_parts.append((cm, b))
    kt = _dot_nt(wkt_ref[...], h).astype(BF16)
    for j in range(cpt):
        kt_ref[j] = kt[:, j * L:(j + 1) * L]
        packed = jnp.concatenate([a[j * H:(j + 1) * H, :] for d in range(2) for a in col_parts[d]]
                                 + [jnp.zeros((L - 4 * H, L), F32)], axis=0)
        cols_ref[j] = _transpose_rows(eye, packed)


def _mlstm_in(x, mod, nw, wq, wkt, wv, wgt, bgt, *, tm):
    b, s, _ = x.shape
    nch, cpt = s // A_CHUNK, tm // A_CHUNK
    ng = 4 * A_HEADS
    est = 2 * tm * D_MODEL * 4 + 2 * (D_MODEL * (2 * QK_A + V_A + V7X_LANES)) * 2 \
        + 2 * tm * (QK_A * 2 * 2 + V_A * 2 + 2 * V7X_LANES * 4) + tm * (2 * QK_A + V_A) * 4 + (4 << 20)
    return pl.pallas_call(
        _mlstm_in_kernel,
        grid=(b, s // tm),
        in_specs=[
            pl.BlockSpec((None, tm, D_MODEL), lambda i, m: (i, m, 0)),
            pl.BlockSpec((None, N_MOD, D_MODEL), lambda i, m: (i, 0, 0)),
            pl.BlockSpec((1, D_MODEL), lambda i, m: (0, 0)),
            _resident(wq.shape), _resident(wkt.shape), _resident(wv.shape),
            _resident(wgt.shape), _resident(bgt.shape),
        ],
        out_specs=[
            pl.BlockSpec((None, tm, QK_A), lambda i, m: (i, m, 0)),
            pl.BlockSpec((None, cpt, QK_A, A_CHUNK), lambda i, m: (i, m, 0, 0)),
            pl.BlockSpec((None, tm, V_A), lambda i, m: (i, m, 0)),
            pl.BlockSpec((None, cpt, 2, ng, A_CHUNK), lambda i, m: (i, m, 0, 0, 0)),
            pl.BlockSpec((None, cpt, A_CHUNK, A_CHUNK), lambda i, m: (i, m, 0, 0)),
        ],
        out_shape=[
            jax.ShapeDtypeStruct((b, s, QK_A), BF16),
            jax.ShapeDtypeStruct((b, nch, QK_A, A_CHUNK), BF16),
            jax.ShapeDtypeStruct((b, s, V_A), BF16),
            jax.ShapeDtypeStruct((b, nch, 2, ng, A_CHUNK), F32),
            jax.ShapeDtypeStruct((b, nch, A_CHUNK, A_CHUNK), F32),
        ],
        compiler_params=_params(("parallel", "parallel"), est),
        name="mlstm_in",
    )(x, mod, nw, wq, wkt, wv, wgt, bgt)


def _split3(a):
    hi = a.astype(BF16)
    r1 = a - hi.astype(F32)
    mid = r1.astype(BF16)
    lo = (r1 - mid.astype(F32)).astype(BF16)
    return hi, mid, lo


def _cumsum_rows(a, tri):
    hi, mid, lo = _split3(a)
    return _dot(hi, tri) + _dot(mid, tri) + _dot(lo, tri)


def _transpose_rows(eye, a):
    hi, mid, lo = _split3(a)
    return _dot_nt(eye, hi) + _dot_nt(eye, mid) + _dot_nt(eye, lo)


def _cummax_lanes(x, lane, reverse):
    n = x.shape[1]
    k = 1
    while k < n:
        if reverse:
            shifted = jnp.where(lane < n - k, pltpu.roll(x, n - k, axis=1), -jnp.inf)
        else:
            shifted = jnp.where(lane >= k, pltpu.roll(x, k, axis=1), -jnp.inf)
        x = jnp.maximum(x, shifted)
        k *= 2
    return x


def _mlstm_scan_kernel(qf_ref, ktf_ref, vf_ref, rowsf_ref, colsf_ref, qb_ref, ktb_ref, vb_ref, rowsb_ref, colsb_ref,
                       hf_ref, hb_ref, c_ref, m_ref, *, cps):
    L = A_CHUNK
    H = A_HEADS

    @pl.when(pl.program_id(1) == 0)
    def _():
        c_ref[...] = jnp.zeros_like(c_ref)
        m_ref[...] = jnp.full(m_ref.shape, -jnp.inf, F32)

    ri = lax.broadcasted_iota(jnp.int32, (L, L), 0)
    ci = lax.broadcasted_iota(jnp.int32, (L, L), 1)
    keeps = (ri >= ci, ri <= ci)
    ones_blk = jnp.ones((L, L), BF16)
    refs = ((qf_ref, ktf_ref, vf_ref, rowsf_ref, colsf_ref, hf_ref),
            (qb_ref, ktb_ref, vb_ref, rowsb_ref, colsb_ref, hb_ref))

    def body(c, carry):
        stages = []
        for d in range(2):
            q_ref, kt_ref, v_ref, rows_ref, cols_ref, out_ref = refs[d]
            cc = c if d == 0 else cps - 1 - c
            r0 = pl.multiple_of(cc * L, L)
            rows = rows_ref[cc, d]
            z, log_w, lw_max, b_last = (rows[k * H:(k + 1) * H, :] for k in range(4))
            cols = cols_ref[cc]
            m_prev = m_ref[d * H:(d + 1) * H, :]
            m_new = jnp.maximum(b_last + m_prev, lw_max)
            w = jnp.exp2(log_w - m_new)
            decay = jnp.exp2((b_last + m_prev) - m_new)
            m_ref[d * H:(d + 1) * H, :] = m_new
            q_all = q_ref[pl.ds(r0, L), :]
            kt_all = kt_ref[cc]
            for hd in range(H):
                stages.append(dict(
                    d=d, sd=d * H + hd, out_ref=out_ref, r0=r0, hd=hd, cols=cols,
                    q=q_all[:, hd * A_DK:(hd + 1) * A_DK], kt=kt_all[hd * A_DK:(hd + 1) * A_DK, :],
                    v_ext=jnp.concatenate([v_ref[pl.ds(r0, L), hd * A_DV:(hd + 1) * A_DV], ones_blk], axis=1),
                    z=z[hd:hd + 1, :], m_prev=m_prev[hd:hd + 1, :], w=w[hd:hd + 1, :],
                    decay=decay[hd:hd + 1, 0:1]))
        for st in stages:
            st["c_prev"] = c_ref[st["sd"]]
            st["qk"] = _dot(st["q"], st["kt"])
        for st in stages:
            lane = 2 * st["d"] * H + st["hd"]
            cm_b = jnp.take_along_axis(st["cols"], jnp.full((L, L), lane, jnp.int32), axis=1,
                                       mode="promise_in_bounds")
            b_b = jnp.take_along_axis(st["cols"], jnp.full((L, L), lane + H, jnp.int32), axis=1,
                                      mode="promise_in_bounds")
            u_b = -jnp.maximum(st["m_prev"], cm_b)
            st["dmat"] = jnp.exp2(jnp.where(keeps[st["d"]], u_b - st["z"], -jnp.inf))
            st["w_inter"] = jnp.exp2(u_b[:, :A_DK] + st["m_prev"][:, :A_DK])
            st["floor"] = jnp.exp2(u_b - b_b)
        for st in stages:
            lhs = jnp.concatenate([(st["qk"] * st["dmat"]).astype(BF16),
                                   (st["w_inter"] * st["q"].astype(F32)).astype(BF16),
                                   jnp.zeros((L, L - A_DK), BF16)], axis=1)
            rhs = jnp.concatenate([st["v_ext"], st["c_prev"].astype(BF16),
                                   jnp.zeros((L - A_DK, 2 * L), BF16)], axis=0)
            st["r"] = _dot(lhs, rhs)
            kw = (st["kt"].astype(F32) * st["w"]).astype(BF16)
            st["upd"] = _dot(kw, st["v_ext"])
        for st in stages:
            r = st["r"]
            floor = st["floor"]
            hd = st["hd"]
            st["out_ref"][pl.ds(st["r0"], L), hd * A_DV:(hd + 1) * A_DV] = \
                r[:, :L] / jnp.maximum(jnp.abs(r[:, L:]), floor)
            c_ref[st["sd"]] = st["decay"] * st["c_prev"] + st["upd"]
        return carry

    lax.fori_loop(0, cps, body, 0)


def _mlstm_scan(q, kt, v, rows, cols, *, tb):
    b, s, _ = q.shape
    cps = tb // A_CHUNK
    nb = s // tb
    ng = 4 * A_HEADS

    def specs(idx):
        return [
            pl.BlockSpec((None, tb, QK_A), lambda i, j: (i, idx(j), 0)),
            pl.BlockSpec((None, cps, QK_A, A_CHUNK), lambda i, j: (i, idx(j), 0, 0)),
            pl.BlockSpec((None, tb, V_A), lambda i, j: (i, idx(j), 0)),
            pl.BlockSpec((None, cps, 2, ng, A_CHUNK), lambda i, j: (i, idx(j), 0, 0, 0)),
            pl.BlockSpec((None, cps, A_CHUNK, A_CHUNK), lambda i, j: (i, idx(j), 0, 0)),
        ]

    fwd = lambda j: j
    bwd = lambda j: nb - 1 - j
    est = 2 * 2 * tb * (QK_A * 2 * 2 + V_A * 2 + V7X_LANES * 4) + 2 * 2 * tb * V_A * 4 \
        + 2 * A_HEADS * A_DK * 2 * A_CHUNK * 4 + (16 << 20)
    return pl.pallas_call(
        functools.partial(_mlstm_scan_kernel, cps=cps),
        grid=(b, nb),
        in_specs=specs(fwd) + specs(bwd),
        out_specs=[
            pl.BlockSpec((None, tb, V_A), lambda i, j: (i, j, 0)),
            pl.BlockSpec((None, tb, V_A), lambda i, j: (i, nb - 1 - j, 0)),
        ],
        out_shape=[jax.ShapeDtypeStruct((b, s, V_A), F32)] * 2,
        scratch_shapes=[
            pltpu.VMEM((2 * A_HEADS, A_DK, 2 * A_CHUNK), F32),
            pltpu.VMEM((2 * A_HEADS, A_CHUNK), F32),
        ],
        compiler_params=_params(("parallel", "arbitrary"), est),
        name="mlstm_scan",
    )(q, kt, v, rows, cols, q, kt, v, rows, cols)


def _mlstm_out_kernel(x_ref, hf_ref, hb_ref, mod_ref, nw_ref, wo_ref, mnw_ref, wout_ref, o_ref):
    tm = x_ref.shape[0]
    slab = tm // MLSTM_OUT_SLABS
    rows = [pl.ds(r * slab, slab) for r in range(MLSTM_OUT_SLABS)]
    o_gates = []
    for rs in rows:
        h = _norm_mod(x_ref[rs, :], nw_ref[...], mod_ref[3:4, :], mod_ref[4:5, :]).astype(BF16)
        o_gates.append(_dot(h, wo_ref[...]))
    for rs, o_gate in zip(rows, o_gates):
        hs = hf_ref[rs, :] + hb_ref[rs, :]
        parts = []
        for hd in range(A_HEADS):
            a = hs[:, hd * A_DV:(hd + 1) * A_DV]
            ms = jnp.mean(a * a, axis=-1, keepdims=True)
            parts.append(a * lax.rsqrt(ms + EPS))
        y = jnp.concatenate(parts, axis=1) * mnw_ref[...]
        z = (y * _sigmoid(o_gate)).astype(BF16)
        o_ref[rs, :] = x_ref[rs, :] + mod_ref[5:6, :] * _dot(z, wout_ref[...])


def _mlstm_out(x, hf, hb, mod, nw, wo, mnw, wout, *, tm):
    b, s, _ = x.shape
    tile = pl.BlockSpec((None, tm, D_MODEL), lambda i, m: (i, m, 0))
    est = 2 * 4 * tm * D_MODEL * 4 + 2 * D_MODEL * D_MODEL * 2 + 6 * tm * D_MODEL * 4
    return pl.pallas_call(
        _mlstm_out_kernel,
        grid=(b, s // tm),
        in_specs=[
            tile, tile, tile,
            pl.BlockSpec((None, N_MOD, D_MODEL), lambda i, m: (i, 0, 0)),
            pl.BlockSpec((1, D_MODEL), lambda i, m: (0, 0)),
            _resident(wo.shape),
            pl.BlockSpec((1, V_A), lambda i, m: (0, 0)),
            _resident(wout.shape),
        ],
        out_specs=tile,
        out_shape=jax.ShapeDtypeStruct(x.shape, F32),
        compiler_params=_params(("parallel", "parallel"), est),
        name="mlstm_out",
    )(x, hf, hb, mod, nw, wo, mnw, wout)


def _attn_in_kernel(x_ref, mod_ref, nw_ref, wq_ref, wqr_ref, wkt_ref, wkrt_ref, wv_ref, seg_ref,
                    cq_ref, sq_ref, ckt_ref, skt_ref, qw_ref, qwsw_ref, kw_ref, kwsw_ref,
                    q_ref, kt_ref, v_ref, *, q_scale):
    tm = x_ref.shape[0]
    slab = tm // ATTN_IN_SLABS
    reps = Q_ATT // V7X_LANES
    for r in range(ATTN_IN_SLABS):
        rs = pl.ds(r * slab, slab)
        cs = slice(r * slab, (r + 1) * slab)
        h = _norm_mod(x_ref[rs, :], nw_ref[...], mod_ref[3:4, :], mod_ref[4:5, :]).astype(BF16)
        q = _dot(h, wq_ref[...])
        q_rot = _dot(h, wqr_ref[...])
        v = _dot(h, wv_ref[...])
        kt = _dot_nt(wkt_ref[...], h)
        kt_rot = _dot_nt(wkrt_ref[...], h)
        ssq = _dot((q * q).astype(BF16), seg_ref[...])

        lane = lax.broadcasted_iota(jnp.int32, v.shape, 1)
        v_ref[rs, :] = jnp.where(lane % V7X_LANES == HEAD_DIM, 1.0, v).astype(BF16)

        cos_k = ckt_ref[:, cs] * kw_ref[...]
        sin_k = skt_ref[:, cs] * kwsw_ref[...]
        outs = []
        for j in range(ATT_KV_HEADS):
            a = kt[j * HEAD_DIM:(j + 1) * HEAD_DIM, :]
            ar = kt_rot[j * HEAD_DIM:(j + 1) * HEAD_DIM, :]
            rk = lax.rsqrt(jnp.mean(a * a, axis=0, keepdims=True) + EPS)
            outs.append(rk * (a * cos_k + ar * sin_k))
        kt_ref[:, cs] = jnp.concatenate(outs, axis=0).astype(BF16)

        rq = lax.rsqrt(ssq * (1.0 / HEAD_DIM) + EPS)
        cos_q = jnp.concatenate([cq_ref[rs, :] * qw_ref[...]] * reps, axis=1)
        sin_q = jnp.concatenate([sq_ref[rs, :] * qwsw_ref[...]] * reps, axis=1)
        q_ref[rs, :] = ((rq * q_scale) * (q * cos_q + q_rot * sin_q)).astype(BF16)


def _attn_in(x, mod, nw, w, tabs, *, tm, tk, q_scale):
    b, s, _ = x.shape
    per = tk // tm
    const2 = lambda i, m: (0, 0)
    est = 2 * tm * D_MODEL * 4 + (3 * D_MODEL * Q_ATT + 3 * D_MODEL * KV_ATT) * 2 \
        + 2 * tm * (Q_ATT + 2 * KV_ATT) * 2 + 8 * tm * Q_ATT * 4 + 8 * tm * V7X_LANES * 4
    return pl.pallas_call(
        functools.partial(_attn_in_kernel, q_scale=q_scale),
        grid=(b, s // tm),
        in_specs=[
            pl.BlockSpec((None, tm, D_MODEL), lambda i, m: (i, m, 0)),
            pl.BlockSpec((None, N_MOD, D_MODEL), lambda i, m: (i, 0, 0)),
            pl.BlockSpec((1, D_MODEL), const2),
            _resident(w["wq"].shape), _resident(w["wqr"].shape),
            _resident(w["wkt"].shape), _resident(w["wkrt"].shape),
            _resident(w["wv"].shape), _resident(w["seg"].shape),
            pl.BlockSpec((tm, V7X_LANES), lambda i, m: (m, 0)),
            pl.BlockSpec((tm, V7X_LANES), lambda i, m: (m, 0)),
            pl.BlockSpec((HEAD_DIM, tm), lambda i, m: (0, m)),
            pl.BlockSpec((HEAD_DIM, tm), lambda i, m: (0, m)),
            pl.BlockSpec((1, V7X_LANES), const2), pl.BlockSpec((1, V7X_LANES), const2),
            pl.BlockSpec((HEAD_DIM, 1), const2), pl.BlockSpec((HEAD_DIM, 1), const2),
        ],
        out_specs=[
            pl.BlockSpec((None, tm, Q_ATT), lambda i, m: (i, m, 0)),
            pl.BlockSpec((None, None, KV_ATT, tm), lambda i, m: (i, m // per, 0, m % per)),
            pl.BlockSpec((None, tm, VX_ATT), lambda i, m: (i, m, 0)),
        ],
        out_shape=[
            jax.ShapeDtypeStruct((b, s, Q_ATT), BF16),
            jax.ShapeDtypeStruct((b, s // tk, KV_ATT, tk), BF16),
            jax.ShapeDtypeStruct((b, s, VX_ATT), BF16),
        ],
        compiler_params=_params(("parallel", "parallel"), est),
        name="attn_in",
    )(x, mod, nw, w["wq"], w["wqr"], w["wkt"], w["wkrt"], w["wv"], w["seg"],
      tabs["cos_q"], tabs["sin_q"], tabs["cos_kt"], tabs["sin_kt"],
      w["qw"], w["qwsw"], w["kw"], w["kwsw"])


def _swa_kernel(bound_ref, sink_ref, q_ref, ktl_ref, ktm_ref, ktr_ref, vl_ref, vm_ref, vr_ref, o_ref,
                *, seq, tq, bounded):
    nsub = tq // BLOCK
    j = pl.program_id(1)
    kt_win = jnp.concatenate([ktl_ref[...], ktm_ref[...], ktr_ref[...]], axis=1)
    v_win = jnp.concatenate([vl_ref[...], vm_ref[...], vr_ref[...]], axis=0)
    qi = lax.broadcasted_iota(jnp.int32, (BLOCK, 3 * BLOCK), 0)
    kj = lax.broadcasted_iota(jnp.int32, (BLOCK, 3 * BLOCK), 1) - BLOCK
    band_bias = jnp.where(jnp.abs(qi - kj) <= WINDOW, 0.0, -jnp.inf)
    biases = []
    for i in range(nsub):
        bias = band_bias
        if i == 0:
            bias = jnp.where(kj + j * tq >= 0, bias, -jnp.inf)
        if i == nsub - 1:
            bias = jnp.where(kj + (j * tq + i * BLOCK) < seq, bias, -jnp.inf)
        biases.append(bias)
    q_tiles = [q_ref[i * BLOCK:(i + 1) * BLOCK, :] for i in range(nsub)]

    per_blk = ATT_HEADS // SWA_STAGE_HEADS
    n_stage = nsub * per_blk

    def heads_of(n):
        i, part = divmod(n, per_blk)
        return i, range(part * SWA_STAGE_HEADS, (part + 1) * SWA_STAGE_HEADS)

    def scores(n):
        i, heads = heads_of(n)
        return [_dot(q_tiles[i][:, h * HEAD_DIM:(h + 1) * HEAD_DIM],
                     kt_win[(h // ATT_GROUP) * HEAD_DIM:(h // ATT_GROUP + 1) * HEAD_DIM, i * BLOCK:(i + 3) * BLOCK])
                + biases[i] for h in heads]

    s_next = scores(0)
    outs = []
    for n in range(n_stage):
        s_heads = s_next
        if n + 1 < n_stage:
            s_next = scores(n + 1)
        i, heads = heads_of(n)
        sinks = [sink_ref[h] * LOG2E for h in heads]
        if bounded:
            shifts = [jnp.maximum(bound_ref[0], sk) for sk in sinks]
            p = jnp.concatenate([jnp.exp2(s - m) for s, m in zip(s_heads, shifts)], axis=0).astype(BF16)
            sink_w = jnp.concatenate([jnp.full((BLOCK, 1), jnp.exp2(sk - m), F32) for sk, m in zip(sinks, shifts)],
                                     axis=0)
        else:
            s = jnp.concatenate(s_heads, axis=0)
            sink = jnp.concatenate([jnp.full((BLOCK, 1), sk, F32) for sk in sinks], axis=0)
            m = jnp.maximum(jnp.max(s, axis=1, keepdims=True), sink)
            p = jnp.exp2(s - m).astype(BF16)
            sink_w = jnp.exp2(sink - m)
        r = jnp.concatenate(
            [_dot(p[a * BLOCK:(a + 1) * BLOCK, :],
                  v_win[i * BLOCK:(i + 3) * BLOCK, (h // ATT_GROUP) * V7X_LANES:(h // ATT_GROUP + 1) * V7X_LANES])
             for a, h in enumerate(heads)], axis=0)
        o = r[:, :HEAD_DIM] / (r[:, HEAD_DIM:HEAD_DIM + 1] + sink_w)
        outs += [o[a * BLOCK:(a + 1) * BLOCK, :] for a in range(SWA_STAGE_HEADS)]
        if len(outs) == ATT_HEADS:
            o_ref[i * BLOCK:(i + 1) * BLOCK, :] = jnp.concatenate(outs, axis=1).astype(BF16)
            outs = []


def _swa(bound, q, kt, v, sink, *, tq, bounded):
    b, s, _ = q.shape
    nsub = tq // BLOCK
    nblk = s // BLOCK
    nq = s // tq
    left = lambda j: jnp.maximum(j * nsub - 1, 0)
    right = lambda j: jnp.minimum((j + 1) * nsub, nblk - 1)
    est = 2 * (tq * Q_ATT * 2 * 2 + (tq + 2 * BLOCK) * (KV_ATT + VX_ATT) * 2) + (12 << 20)
    return pl.pallas_call(
        functools.partial(_swa_kernel, seq=s, tq=tq, bounded=bounded),
        grid=(b, nq),
        in_specs=[
            pl.BlockSpec(memory_space=pltpu.SMEM),
            pl.BlockSpec(memory_space=pltpu.SMEM),
            pl.BlockSpec((None, tq, Q_ATT), lambda i, j: (i, j, 0)),
            pl.BlockSpec((None, None, KV_ATT, BLOCK), lambda i, j: (i, 0, 0, left(j))),
            pl.BlockSpec((None, None, KV_ATT, tq), lambda i, j: (i, 0, 0, j)),
            pl.BlockSpec((None, None, KV_ATT, BLOCK), lambda i, j: (i, 0, 0, right(j))),
            pl.BlockSpec((None, BLOCK, VX_ATT), lambda i, j: (i, left(j), 0)),
            pl.BlockSpec((None, tq, VX_ATT), lambda i, j: (i, j, 0)),
            pl.BlockSpec((None, BLOCK, VX_ATT), lambda i, j: (i, right(j), 0)),
        ],
        out_specs=pl.BlockSpec((None, tq, Q_ATT), lambda i, j: (i, j, 0)),
        out_shape=jax.ShapeDtypeStruct((b, s, Q_ATT), BF16),
        compiler_params=_params(("parallel", "parallel"), est),
        name="swa_bounded" if bounded else "swa",
    )(bound, sink, q, kt, kt, kt, v, v, v)


def _axial_kernel(bound_ref, q_ref, kt_ref, v_ref, o_ref, m_ref, acc_ref, s_ref, *, tk, nk, bounded):
    q_rows = q_ref[...]
    qs = [q_rows[:, h * HEAD_DIM:(h + 1) * HEAD_DIM] for h in range(ATT_HEADS)]
    m_ref[...] = jnp.full(m_ref.shape, -jnp.inf, F32)
    acc_ref[...] = jnp.zeros_like(acc_ref)

    def scores(h, kt_c):
        g = h // ATT_GROUP
        return _dot(qs[h], kt_c[g * HEAD_DIM:(g + 1) * HEAD_DIM, :])

    kt_0 = kt_ref[0]
    for h in range(AXIAL_AHEAD):
        s_ref[h] = scores(h, kt_0)

    def body(c, carry):
        kt_c = kt_ref[c]
        kt_n = kt_ref[jnp.minimum(c + 1, nk - 1)]
        v_c = v_ref[pl.ds(pl.multiple_of(c * tk, tk), tk), :]
        pending = {h: s_ref[h] for h in range(AXIAL_AHEAD)}
        for h in range(ATT_HEADS):
            ahead = h + AXIAL_AHEAD
            if ahead < ATT_HEADS:
                pending[ahead] = scores(ahead, kt_c)
            else:
                s_ref[ahead - ATT_HEADS] = scores(ahead - ATT_HEADS, kt_n)
            g = h // ATT_GROUP
            s = pending.pop(h)
            v_ext = v_c[:, g * V7X_LANES:(g + 1) * V7X_LANES]
            if bounded:
                acc_ref[h] += _dot(jnp.exp2(s - bound_ref[0]).astype(BF16), v_ext)
            else:
                m_prev = m_ref[h]
                m_new = jnp.maximum(m_prev, jnp.max(s, axis=1, keepdims=True))
                p = jnp.exp2(s - m_new).astype(BF16)
                acc_ref[h] = jnp.exp2(m_prev - m_new) * acc_ref[h] + _dot(p, v_ext)
                m_ref[h] = m_new
        return carry

    lax.fori_loop(0, nk, body, 0, unroll=min(nk, AXIAL_UNROLL))
    outs = []
    for h in range(ATT_HEADS):
        acc = acc_ref[h]
        outs.append(acc[:, :HEAD_DIM] / acc[:, HEAD_DIM:HEAD_DIM + 1])
    o_ref[...] = jnp.concatenate(outs, axis=1).astype(BF16)


def _axial(bound, q, kt, v, *, tq, tk, bounded):
    b, s, _ = q.shape
    nk = s // tk
    rows = ATT_GROUP * tq
    est = 2 * tq * Q_ATT * 2 * 2 + s * (KV_ATT + VX_ATT) * 2 \
        + ATT_KV_HEADS * (2 * rows * V7X_LANES * 4 + rows * tk * 6) + (4 << 20)
    return pl.pallas_call(
        functools.partial(_axial_kernel, tk=tk, nk=nk, bounded=bounded),
        grid=(b, s // tq),
        in_specs=[
            pl.BlockSpec(memory_space=pltpu.SMEM),
            pl.BlockSpec((None, tq, Q_ATT), lambda i, j: (i, j, 0)),
            pl.BlockSpec((None, nk, KV_ATT, tk), lambda i, j: (i, 0, 0, 0), pipeline_mode=pl.Buffered(1)),
            pl.BlockSpec((None, s, VX_ATT), lambda i, j: (i, 0, 0), pipeline_mode=pl.Buffered(1)),
        ],
        out_specs=pl.BlockSpec((None, tq, Q_ATT), lambda i, j: (i, j, 0)),
        out_shape=jax.ShapeDtypeStruct((b, s, Q_ATT), BF16),
        scratch_shapes=[
            pltpu.VMEM((ATT_HEADS, tq, 1), F32), pltpu.VMEM((ATT_HEADS, tq, V7X_LANES), F32),
            pltpu.VMEM((AXIAL_AHEAD, tq, tk), F32),
        ],
        compiler_params=_params(("parallel", "parallel"), est),
        name="axial_bounded" if bounded else "axial",
    )(bound, q, kt, v)


def _rope_tables(pos, dim):
    inv = ROPE_THETA ** (-jnp.arange(0, dim, 2, dtype=F32) / dim)
    ang = pos.astype(F32)[:, None] * inv[None, :]
    ang = jnp.concatenate([ang, ang], axis=-1)
    return jnp.cos(ang), jnp.sin(ang)


def _rot_half_perm(widths):
    perm, sign, base = [], [], 0
    for w in widths:
        half = w // 2
        perm += [base + half + i for i in range(half)] + [base + i for i in range(half)]
        sign += [-1.0] * half + [1.0] * half
        base += w
    return jnp.array(perm, jnp.int32), jnp.array(sign, F32)


def _attn_tables(cos, sin):
    reps = V7X_LANES // HEAD_DIM
    return {
        "cos_q": jnp.tile(cos, (1, reps)), "sin_q": jnp.tile(sin, (1, reps)),
        "cos_kt": cos.T, "sin_kt": sin.T,
    }


def _attn_weights(w_in, q_norm, k_norm, widths):
    perm, sign = _rot_half_perm(widths)
    wq = w_in[:, :Q_ATT]
    wk = w_in[:, Q_ATT:Q_ATT + KV_ATT]
    wv = w_in[:, Q_ATT + KV_ATT:]

    def rotated(w, heads):
        w3 = w.reshape(D_MODEL, heads, HEAD_DIM)
        return (w3[:, :, perm] * sign).reshape(D_MODEL, heads * HEAD_DIM)

    seg_id = jnp.arange(Q_ATT) // HEAD_DIM
    reps = V7X_LANES // HEAD_DIM
    return {
        "wq": wq.astype(BF16), "wqr": rotated(wq, ATT_HEADS).astype(BF16),
        "wkt": wk.T.astype(BF16), "wkrt": rotated(wk, ATT_KV_HEADS).T.astype(BF16),
        "wv": jnp.pad(wv.reshape(D_MODEL, ATT_KV_HEADS, HEAD_DIM),
                      ((0, 0), (0, 0), (0, V7X_LANES - HEAD_DIM))).reshape(D_MODEL, VX_ATT).astype(BF16),
        "seg": (seg_id[:, None] == seg_id[None, :]).astype(BF16),
        "qw": jnp.tile(q_norm, reps)[None, :], "qwsw": jnp.tile(q_norm[perm], reps)[None, :],
        "kw": k_norm[:, None], "kwsw": k_norm[perm][:, None],
        "bound": (BOUND_SLACK * LOG2E * HEAD_DIM ** 0.5
                  * jnp.max(jnp.abs(q_norm)) * jnp.max(jnp.abs(k_norm))).reshape(1).astype(F32),
    }


def _ffn_weights(w13, w2):
    def chunked(w):
        return w.reshape(D_MODEL, N_FFN_CHUNKS, FFN_CHUNK).transpose(1, 0, 2).astype(BF16)
    return chunked(w13[:, :D_FF]), chunked(w13[:, D_FF:]), w2.reshape(N_FFN_CHUNKS, FFN_CHUNK, D_MODEL).astype(BF16)


def _mlstm_weights(w_in, b_gate, norm_w, w_out):
    wq = w_in[:, :QK_A]
    wk = w_in[:, QK_A:2 * QK_A]
    wv = w_in[:, 2 * QK_A:2 * QK_A + V_A]
    wo = w_in[:, 2 * QK_A + V_A:2 * QK_A + 2 * V_A]
    wg = w_in[:, 2 * QK_A + 2 * V_A:]
    return {
        "wq": wq.astype(BF16), "wkt": wk.T.astype(BF16), "wv": wv.astype(BF16), "wo": wo.astype(BF16),
        "wgt": wg.T.astype(BF16), "bgt": b_gate[:, None],
        "mnw": norm_w[None, :], "wout": w_out.astype(BF16),
    }


def _tile(s, want):
    t = min(s, want)
    assert s % t == 0
    return t


def _bounded_or_exact(attend, bound, *operands):
    return lax.cond(bound[0] <= SCORE_BOUND_MAX,
                    functools.partial(attend, bounded=True), functools.partial(attend, bounded=False),
                    bound, *operands)


def _trunk(x, mod_all, prep):
    b, s, _ = x.shape
    tm_ffn = _tile(s, 1024)
    tm = _tile(s, 512)
    for i in range(DEPTH):
        mod = mod_all[i].reshape(b, N_MOD, D_MODEL)
        lw = prep["layers"][i]
        x = _ffn(x, mod, lw["nw"][0:1], *lw["ffn0"], row0=0, tm=tm_ffn)
        kind = i % N_MIXERS
        mw = lw["mixer"]
        if kind == 0:
            q, kt, v, rows, cols = _mlstm_in(x, mod, lw["nw"][1:2], mw["wq"], mw["wkt"], mw["wv"],
                                             mw["wgt"], mw["bgt"], tm=tm)
            hf, hb = _mlstm_scan(q, kt, v, rows, cols, tb=_tile(s, 512))
            x = _mlstm_out(x, hf, hb, mod, lw["nw"][1:2], mw["wo"], mw["mnw"], mw["wout"], tm=tm)
        elif kind == 1:
            q, kt, v = _attn_in(x, mod, lw["nw"][1:2], mw, prep["swa_tabs"][s], tm=tm, tk=s,
                                q_scale=HEAD_DIM ** -0.5 * LOG2E)
            attn = _bounded_or_exact(functools.partial(_swa, tq=_tile(s, 512)), mw["bound"], q, kt, v, mw["sink"])
        else:
            tk = _tile(s, 1024)
            q, kt, v = _attn_in(x, mod, lw["nw"][1:2], mw, prep["axial_tabs"][s], tm=tm, tk=tk,
                                q_scale=HEAD_DIM ** -0.5 * LOG2E)
            attn = _bounded_or_exact(functools.partial(_axial, tq=_tile(s, 128), tk=tk), mw["bound"], q, kt, v)
        if kind == 0:
            x = _ffn(x, mod, lw["nw"][2:3], *lw["ffn1"], row0=6, tm=tm_ffn)
        else:
            x = _ffn(x, mod, lw["nw"][2:3], *lw["ffn1"], row0=6, tm=tm_ffn, attn=attn, wout=mw["wout"])
    return x


def _prepare(seqs, ffn_w13, ffn_w2, norm_w,
             mlstm_w_in, mlstm_b_gate, mlstm_norm_w, mlstm_w_out,
             swa_w_in, swa_q_norm, swa_k_norm, swa_sink, swa_w_out,
             axial_w_in, axial_q_norm, axial_k_norm, axial_w_out):
    layers = []
    for i in range(DEPTH):
        kind, j = i % N_MIXERS, i // N_MIXERS
        if kind == 0:
            mixer = _mlstm_weights(mlstm_w_in[j], mlstm_b_gate[j], mlstm_norm_w[j], mlstm_w_out[j])
        elif kind == 1:
            mixer = _attn_weights(swa_w_in[j], swa_q_norm[j], swa_k_norm[j], (HEAD_DIM,))
            mixer["sink"] = swa_sink[j]
            mixer["wout"] = swa_w_out[j].astype(BF16)
        else:
            mixer = _attn_weights(axial_w_in[j], axial_q_norm[j], axial_k_norm[j], (HEAD_DIM // 2, HEAD_DIM // 2))
            mixer["wout"] = axial_w_out[j].astype(BF16)
        layers.append({
            "nw": norm_w[i],
            "ffn0": _ffn_weights(ffn_w13[i, 0], ffn_w2[i, 0]),
            "ffn1": _ffn_weights(ffn_w13[i, 1], ffn_w2[i, 1]),
            "mixer": mixer,
        })
    swa_tabs, axial_tabs = {}, {}
    for s in seqs:
        swa_tabs[s] = _attn_tables(*_rope_tables(jnp.arange(s), HEAD_DIM))
        rows = s // GRID_W
        row_ids = jnp.repeat(jnp.arange(rows), GRID_W)
        col_ids = jnp.tile(jnp.arange(GRID_W), rows)
        rc, rs = _rope_tables(row_ids, HEAD_DIM // 2)
        cc, cs = _rope_tables(col_ids, HEAD_DIM // 2)
        axial_tabs[s] = _attn_tables(jnp.concatenate([rc, cc], axis=-1), jnp.concatenate([rs, cs], axis=-1))
    return {"layers": layers, "swa_tabs": swa_tabs, "axial_tabs": axial_tabs}


def kernel(x_prompt, x_sample, c_prompt, c_sample, ffn_w13, ffn_w2, ada_w, ada_b, norm_w, mlstm_w_in, mlstm_b_gate, mlstm_norm_w, mlstm_w_out, swa_w_in, swa_q_norm, swa_k_norm, swa_sink, swa_w_out, axial_w_in, axial_q_norm, axial_k_norm, axial_w_out):
    prep = _prepare({x_prompt.shape[1], x_sample.shape[1]}, ffn_w13, ffn_w2, norm_w,
                    mlstm_w_in, mlstm_b_gate, mlstm_norm_w, mlstm_w_out,
                    swa_w_in, swa_q_norm, swa_k_norm, swa_sink, swa_w_out,
                    axial_w_in, axial_q_norm, axial_k_norm, axial_w_out)
    nb_prompt = c_prompt.shape[0]
    mod_all = _ada_mod(jnp.concatenate([c_prompt, c_sample], axis=0), ada_w, ada_b)
    y_prompt = _trunk(x_prompt, mod_all[:, :nb_prompt], prep)
    y_sample = _trunk(x_sample, mod_all[:, nb_prompt:], prep)
    return (y_prompt, y_sample)
```

```python
import functools

import jax
import jax.numpy as jnp
from jax import lax
from jax.experimental import pallas as pl
from jax.experimental.pallas import tpu as pltpu

F32 = jnp.float32
BF16 = jnp.bfloat16

D_MODEL = 1024
DEPTH = 4
N_MIXERS = 3
D_FF = 2816
EPS = 1e-6
N_MOD = 9
A_HEADS = 8
A_DK = 64
A_DV = 128
A_CHUNK = 128
A_GATE_CAP = 15.0
ATT_HEADS = 16
ATT_KV_HEADS = 4
ATT_GROUP = ATT_HEADS // ATT_KV_HEADS
HEAD_DIM = 64
WINDOW = 128
BLOCK = 128
ROPE_THETA = 10000.0
GRID_W = 64

V7X_VMEM_BYTES = 64 * 1024 * 1024
V7X_LANES = 128
V7X_MXU_DIM = 256

FFN_CHUNK = V7X_MXU_DIM
N_FFN_CHUNKS = D_FF // FFN_CHUNK
QK_A = A_HEADS * A_DK
V_A = A_HEADS * A_DV
Q_ATT = ATT_HEADS * HEAD_DIM
KV_ATT = ATT_KV_HEADS * HEAD_DIM
VX_ATT = ATT_KV_HEADS * V7X_LANES
LOG2E = 1.4426950408889634
AXIAL_AHEAD = 3
AXIAL_UNROLL = 4
SWA_AHEAD = 2
BOUND_SLACK = 1.02
SCORE_BOUND_MAX = 48.0
SWA_STAGE_HEADS = 2
MLSTM_OUT_SLABS = 2

NT_DIMS = (((1,), (1,)), ((), ()))


def _vmem_limit(est_bytes):
    return int(min(est_bytes * 5 // 4 + (4 << 20), V7X_VMEM_BYTES - (4 << 20)))


def _params(sem, est_bytes):
    return pltpu.CompilerParams(dimension_semantics=sem, vmem_limit_bytes=_vmem_limit(est_bytes))


def _resident(shape):
    nd = len(shape)
    return pl.BlockSpec(shape, lambda *_: (0,) * nd, pipeline_mode=pl.Buffered(1))


def _dot(a, b):
    return jnp.dot(a, b, preferred_element_type=F32)


def _dot_nt(a, b):
    return lax.dot_general(a, b, NT_DIMS, preferred_element_type=F32)


def _norm_mod(x, nw, shift, scale):
    ms = jnp.mean(x * x, axis=-1, keepdims=True)
    y = x * lax.rsqrt(ms + EPS)
    return (y * nw) * (1.0 + scale) + shift


def _sigmoid(x):
    return 1.0 / (1.0 + jnp.exp(-x))


def _ada_kernel(c_ref, w_ref, b_ref, o_ref):
    c = c_ref[...]
    a = c * _sigmoid(c)
    o_ref[...] = _dot(a, w_ref[...]) + b_ref[...]


def _ada_mod(c_all, ada_w, ada_b):
    bc = c_all.shape[0]
    n_out = N_MOD * D_MODEL
    tn = D_MODEL
    return pl.pallas_call(
        _ada_kernel,
        grid=(DEPTH, n_out // tn),
        in_specs=[
            pl.BlockSpec((bc, D_MODEL), lambda i, n: (0, 0)),
            pl.BlockSpec((None, D_MODEL, tn), lambda i, n: (i, 0, n)),
            pl.BlockSpec((None, 1, tn), lambda i, n: (i, 0, n)),
        ],
        out_specs=pl.BlockSpec((None, bc, tn), lambda i, n: (i, 0, n)),
        out_shape=jax.ShapeDtypeStruct((DEPTH, bc, n_out), F32),
        compiler_params=_params(("parallel", "parallel"), 4 * (2 * D_MODEL * tn + 4 * bc * tn)),
        name="ada_mod",
    )(c_all, ada_w, ada_b.reshape(DEPTH, 1, n_out))


def _ffn_kernel(*refs, row0, mixer_out):
    if mixer_out:
        x_ref, attn_ref, wout_ref, mod_ref, nw_ref, w13_ref, w2_ref, o_ref, h_ref, a_ref = refs
        o_ref[...] = x_ref[...] + mod_ref[5:6, :] * _dot(attn_ref[...], wout_ref[...])
        x_ref = o_ref
    else:
        x_ref, mod_ref, nw_ref, w13_ref, w2_ref, o_ref, h_ref, a_ref = refs
    shift = mod_ref[row0:row0 + 1, :]
    scale = mod_ref[row0 + 1:row0 + 2, :]
    gate = mod_ref[row0 + 2:row0 + 3, :]
    h_ref[...] = _norm_mod(x_ref[...], nw_ref[...], shift, scale).astype(BF16)

    for c in range(N_FFN_CHUNKS):
        cols = slice(c * FFN_CHUNK, (c + 1) * FFN_CHUNK)
        h = h_ref[...]
        g = _dot(h, w13_ref[:, cols])
        u = _dot(h, w13_ref[:, D_FF + c * FFN_CHUNK:D_FF + (c + 1) * FFN_CHUNK])
        a_ref[:, cols] = ((g * _sigmoid(g)) * u).astype(BF16)
    o_ref[...] = x_ref[...] + (0.5 * gate) * _dot(a_ref[...], w2_ref[...])


def _ffn(x, mod, nw, w13, w2, *, row0, tm, attn=None, wout=None):
    b, s, _ = x.shape
    tile = pl.BlockSpec((None, tm, D_MODEL), lambda i, m: (i, m, 0))
    mixer_out = attn is not None
    est = 4 * tm * D_MODEL * 4 + 3 * D_MODEL * D_FF * 2 + tm * (D_MODEL + D_FF) * 2 + 4 * tm * FFN_CHUNK * 4 \
        + tm * D_MODEL * 4
    mixer_specs, mixer_args = [], []
    if mixer_out:
        est += 2 * tm * D_MODEL * 2 + D_MODEL * D_MODEL * 2
        mixer_specs, mixer_args = [tile, _resident(wout.shape)], [attn, wout]
    return pl.pallas_call(
        functools.partial(_ffn_kernel, row0=row0, mixer_out=mixer_out),
        grid=(b, s // tm),
        in_specs=[tile] + mixer_specs + [
            pl.BlockSpec((None, N_MOD, D_MODEL), lambda i, m: (i, 0, 0)),
            pl.BlockSpec((1, D_MODEL), lambda i, m: (0, 0)),
            _resident(w13.shape),
            _resident(w2.shape),
        ],
        out_specs=tile,
        out_shape=jax.ShapeDtypeStruct(x.shape, F32),
        scratch_shapes=[pltpu.VMEM((tm, D_MODEL), BF16), pltpu.VMEM((tm, D_FF), BF16)],
        compiler_params=_params(("parallel", "parallel"), est),
        name="ffn_mixer_out" if mixer_out else "ffn",
    )(x, *mixer_args, mod, nw, w13, w2)


def _log_sigmoid(x):
    return jnp.minimum(x, 0.0) - jnp.log1p(jnp.exp(-jnp.abs(x)))


def _gate_act(g, is_forget):
    g = A_GATE_CAP * jnp.tanh(g / A_GATE_CAP)
    return jnp.where(is_forget, _log_sigmoid(g), g)


def _mlstm_in_kernel(x_ref, mod_ref, nw_ref, wq_ref, wkt_ref, wv_ref, wgt_ref, bgt_ref,
                     q_ref, kt_ref, v_ref, rows_ref, cols_ref):
    L, H = A_CHUNK, A_HEADS
    cpt = kt_ref.shape[0]
    h = _norm_mod(x_ref[...], nw_ref[...], mod_ref[3:4, :], mod_ref[4:5, :]).astype(BF16)
    g_row = _dot_nt(wgt_ref[...], h) + bgt_ref[...]
    q_ref[...] = (_dot(h, wq_ref[...]) * (A_DK ** -0.5)).astype(BF16)

    row_id = lax.broadcasted_iota(jnp.int32, g_row.shape, 0)
    g_row = _gate_act(g_row, (row_id // H) % 2 == 1) * LOG2E
    ri = lax.broadcasted_iota(jnp.int32, (L, L), 0)
    ci = lax.broadcasted_iota(jnp.int32, (L, L), 1)
    tris = (jnp.where(ri <= ci, 1.0, 0.0).astype(BF16), jnp.where(ri >= ci, 1.0, 0.0).astype(BF16))
    eye = jnp.where(ri == ci, 1.0, 0.0).astype(BF16)
    lane = lax.broadcasted_iota(jnp.int32, (cpt * H, L), 1)

    def chunk_rows(first_row):
        return jnp.concatenate([g_row[first_row:first_row + H, j * L:(j + 1) * L] for j in range(cpt)], axis=0)

    igs = [chunk_rows(2 * d * H) for d in range(2)]
    bs = [_cumsum_rows(chunk_rows((2 * d + 1) * H), tris[d]) for d in range(2)]
    v_ref[...] = _dot(h, wv_ref[...]).astype(BF16)

    col_parts = []
    for d in range(2):
        b, ig = bs[d], igs[d]
        z = b - ig
        cm = _cummax_lanes(-z, lane, reverse=(d == 1))
        b_last = b[:, L - 1:L] if d == 0 else b[:, 0:1]
        log_w = (b_last - b) + ig
        lw_max = jnp.broadcast_to(jnp.max(log_w, axis=1, keepdims=True), b.shape)
        b_last = jnp.broadcast_to(b_last, b.shape)
        for j in range(cpt):
            rows_ref[j, d] = jnp.concatenate([a[j * H:(j + 1) * H, :] for a in (z, log_w, lw_max, b_last)], axis=0)
        col_parts.append((cm, b))
    kt = _dot_nt(wkt_ref[...], h).astype(BF16)
    for j in range(cpt):
        kt_ref[j] = kt[:, j * L:(j + 1) * L]
        packed = jnp.concatenate([a[j * H:(j + 1) * H, :] for d in range(2) for a in col_parts[d]]
                                 + [jnp.zeros((L - 4 * H, L), F32)], axis=0)
        cols_ref[j] = _transpose_rows(eye, packed)


def _mlstm_in(x, mod, nw, wq, wkt, wv, wgt, bgt, *, tm):
    b, s, _ = x.shape
    nch, cpt = s // A_CHUNK, tm // A_CHUNK
    ng = 4 * A_HEADS
    est = 2 * tm * D_MODEL * 4 + 2 * (D_MODEL * (2 * QK_A + V_A + V7X_LANES)) * 2 \
        + 2 * tm * (QK_A * 2 * 2 + V_A * 2 + 2 * V7X_LANES * 4) + tm * (2 * QK_A + V_A) * 4 + (4 << 20)
    return pl.pallas_call(
        _mlstm_in_kernel,
        grid=(b, s // tm),
        in_specs=[
            pl.BlockSpec((None, tm, D_MODEL), lambda i, m: (i, m, 0)),
            pl.BlockSpec((None, N_MOD, D_MODEL), lambda i, m: (i, 0, 0)),
            pl.BlockSpec((1, D_MODEL), lambda i, m: (0, 0)),
            _resident(wq.shape), _resident(wkt.shape), _resident(wv.shape),
            _resident(wgt.shape), _resident(bgt.shape),
        ],
        out_specs=[
            pl.BlockSpec((None, tm, QK_A), lambda i, m: (i, m, 0)),
            pl.BlockSpec((None, cpt, QK_A, A_CHUNK), lambda i, m: (i, m, 0, 0)),
            pl.BlockSpec((None, tm, V_A), lambda i, m: (i, m, 0)),
            pl.BlockSpec((None, cpt, 2, ng, A_CHUNK), lambda i, m: (i, m, 0, 0, 0)),
            pl.BlockSpec((None, cpt, A_CHUNK, A_CHUNK), lambda i, m: (i, m, 0, 0)),
        ],
        out_shape=[
            jax.ShapeDtypeStruct((b, s, QK_A), BF16),
            jax.ShapeDtypeStruct((b, nch, QK_A, A_CHUNK), BF16),
            jax.ShapeDtypeStruct((b, s, V_A), BF16),
            jax.ShapeDtypeStruct((b, nch, 2, ng, A_CHUNK), F32),
            jax.ShapeDtypeStruct((b, nch, A_CHUNK, A_CHUNK), F32),
        ],
        compiler_params=_params(("parallel", "parallel"), est),
        name="mlstm_in",
    )(x, mod, nw, wq, wkt, wv, wgt, bgt)


def _split3(a):
    hi = a.astype(BF16)
    r1 = a - hi.astype(F32)
    mid = r1.astype(BF16)
    lo = (r1 - mid.astype(F32)).astype(BF16)
    return hi, mid, lo


def _cumsum_rows(a, tri):
    hi, mid, lo = _split3(a)
    return _dot(hi, tri) + _dot(mid, tri) + _dot(lo, tri)


def _transpose_rows(eye, a):
    hi, mid, lo = _split3(a)
    return _dot_nt(eye, hi) + _dot_nt(eye, mid) + _dot_nt(eye, lo)


def _cummax_lanes(x, lane, reverse):
    n = x.shape[1]
    k = 1
    while k < n:
        if reverse:
            shifted = jnp.where(lane < n - k, pltpu.roll(x, n - k, axis=1), -jnp.inf)
        else:
            shifted = jnp.where(lane >= k, pltpu.roll(x, k, axis=1), -jnp.inf)
        x = jnp.maximum(x, shifted)
        k *= 2
    return x


def _mlstm_scan_kernel(qf_ref, ktf_ref, vf_ref, rowsf_ref, colsf_ref, qb_ref, ktb_ref, vb_ref, rowsb_ref, colsb_ref,
                       hf_ref, hb_ref, c_ref, m_ref, *, cps):
    L = A_CHUNK
    H = A_HEADS

    @pl.when(pl.program_id(1) == 0)
    def _():
        c_ref[...] = jnp.zeros_like(c_ref)
        m_ref[...] = jnp.full(m_ref.shape, -jnp.inf, F32)

    ri = lax.broadcasted_iota(jnp.int32, (L, L), 0)
    ci = lax.broadcasted_iota(jnp.int32, (L, L), 1)
    keeps = (ri >= ci, ri <= ci)
    ones_blk = jnp.ones((L, L), BF16)
    refs = ((qf_ref, ktf_ref, vf_ref, rowsf_ref, colsf_ref, hf_ref),
            (qb_ref, ktb_ref, vb_ref, rowsb_ref, colsb_ref, hb_ref))

    def body(c, carry):
        stages = []
        for d in range(2):
            q_ref, kt_ref, v_ref, rows_ref, cols_ref, out_ref = refs[d]
            cc = c if d == 0 else cps - 1 - c
            r0 = pl.multiple_of(cc * L, L)
            rows = rows_ref[cc, d]
            z, log_w, lw_max, b_last = (rows[k * H:(k + 1) * H, :] for k in range(4))
            cols = cols_ref[cc]
            m_prev = m_ref[d * H:(d + 1) * H, :]
            m_new = jnp.maximum(b_last + m_prev, lw_max)
            w = jnp.exp2(log_w - m_new)
            decay = jnp.exp2((b_last + m_prev) - m_new)
            m_ref[d * H:(d + 1) * H, :] = m_new
            q_all = q_ref[pl.ds(r0, L), :]
            kt_all = kt_ref[cc]
            for hd in range(H):
                stages.append(dict(
                    d=d, sd=d * H + hd, out_ref=out_ref, r0=r0, hd=hd, cols=cols,
                    q=q_all[:, hd * A_DK:(hd + 1) * A_DK], kt=kt_all[hd * A_DK:(hd + 1) * A_DK, :],
                    v_ext=jnp.concatenate([v_ref[pl.ds(r0, L), hd * A_DV:(hd + 1) * A_DV], ones_blk], axis=1),
                    z=z[hd:hd + 1, :], m_prev=m_prev[hd:hd + 1, :], w=w[hd:hd + 1, :],
                    decay=decay[hd:hd + 1, 0:1]))
        for st in stages:
            st["c_prev"] = c_ref[st["sd"]]
            st["qk"] = _dot(st["q"], st["kt"])
        for st in stages:
            lane = 2 * st["d"] * H + st["hd"]
            cm_b = jnp.take_along_axis(st["cols"], jnp.full((L, L), lane, jnp.int32), axis=1,
                                       mode="promise_in_bounds")
            b_b = jnp.take_along_axis(st["cols"], jnp.full((L, L), lane + H, jnp.int32), axis=1,
                                      mode="promise_in_bounds")
            u_b = -jnp.maximum(st["m_prev"], cm_b)
            st["dmat"] = jnp.exp2(jnp.where(keeps[st["d"]], u_b - st["z"], -jnp.inf))
            st["w_inter"] = jnp.exp2(u_b[:, :A_DK] + st["m_prev"][:, :A_DK])
            st["floor"] = jnp.exp2(u_b - b_b)
        for st in stages:
            lhs = jnp.concatenate([(st["qk"] * st["dmat"]).astype(BF16),
                                   (st["w_inter"] * st["q"].astype(F32)).astype(BF16)], axis=1)
            rhs = jnp.concatenate([st["v_ext"], st["c_prev"].astype(BF16)], axis=0)
            st["r"] = _dot(lhs, rhs)
            kw = (st["kt"].astype(F32) * st["w"]).astype(BF16)
            st["upd"] = _dot(kw, st["v_ext"])
        for st in stages:
            r = st["r"]
            floor = st["floor"]
            hd = st["hd"]
            st["out_ref"][pl.ds(st["r0"], L), hd * A_DV:(hd + 1) * A_DV] = \
                (r[:, :L] / jnp.maximum(jnp.abs(r[:, L:]), floor)).astype(BF16)
            c_ref[st["sd"]] = st["decay"] * st["c_prev"] + st["upd"]
        return carry

    lax.fori_loop(0, cps, body, 0)


def _mlstm_scan(q, kt, v, rows, cols, *, tb):
    b, s, _ = q.shape
    cps = tb // A_CHUNK
    nb = s // tb
    ng = 4 * A_HEADS

    def specs(idx):
        return [
            pl.BlockSpec((None, tb, QK_A), lambda i, j: (i, idx(j), 0)),
            pl.BlockSpec((None, cps, QK_A, A_CHUNK), lambda i, j: (i, idx(j), 0, 0)),
            pl.BlockSpec((None, tb, V_A), lambda i, j: (i, idx(j), 0)),
            pl.BlockSpec((None, cps, 2, ng, A_CHUNK), lambda i, j: (i, idx(j), 0, 0, 0)),
            pl.BlockSpec((None, cps, A_CHUNK, A_CHUNK), lambda i, j: (i, idx(j), 0, 0)),
        ]

    fwd = lambda j: j
    bwd = lambda j: nb - 1 - j
    est = 2 * 2 * tb * (QK_A * 2 * 2 + V_A * 2 + V7X_LANES * 4) + 2 * 2 * tb * V_A * 4 \
        + 2 * A_HEADS * A_DK * 2 * A_CHUNK * 4 + (16 << 20)
    return pl.pallas_call(
        functools.partial(_mlstm_scan_kernel, cps=cps),
        grid=(b, nb),
        in_specs=specs(fwd) + specs(bwd),
        out_specs=[
            pl.BlockSpec((None, tb, V_A), lambda i, j: (i, j, 0)),
            pl.BlockSpec((None, tb, V_A), lambda i, j: (i, nb - 1 - j, 0)),
        ],
        out_shape=[jax.ShapeDtypeStruct((b, s, V_A), BF16)] * 2,
        scratch_shapes=[
            pltpu.VMEM((2 * A_HEADS, A_DK, 2 * A_CHUNK), F32),
            pltpu.VMEM((2 * A_HEADS, A_CHUNK), F32),
        ],
        compiler_params=_params(("parallel", "arbitrary"), est),
        name="mlstm_scan",
    )(q, kt, v, rows, cols, q, kt, v, rows, cols)


def _mlstm_out_kernel(x_ref, hf_ref, hb_ref, mod_ref, nw_ref, wo_ref, mnw_ref, wout_ref, o_ref):
    tm = x_ref.shape[0]
    slab = tm // MLSTM_OUT_SLABS
    rows = [pl.ds(r * slab, slab) for r in range(MLSTM_OUT_SLABS)]
    o_gates = []
    for rs in rows:
        h = _norm_mod(x_ref[rs, :], nw_ref[...], mod_ref[3:4, :], mod_ref[4:5, :]).astype(BF16)
        o_gates.append(_dot(h, wo_ref[...]))
    for rs, o_gate in zip(rows, o_gates):
        hs = hf_ref[rs, :].astype(F32) + hb_ref[rs, :].astype(F32)
        parts = []
        for hd in range(A_HEADS):
            a = hs[:, hd * A_DV:(hd + 1) * A_DV]
            ms = jnp.mean(a * a, axis=-1, keepdims=True)
            parts.append(a * lax.rsqrt(ms + EPS))
        y = jnp.concatenate(parts, axis=1) * mnw_ref[...]
        z = (y * _sigmoid(o_gate)).astype(BF16)
        o_ref[rs, :] = x_ref[rs, :] + mod_ref[5:6, :] * _dot(z, wout_ref[...])


def _mlstm_out(x, hf, hb, mod, nw, wo, mnw, wout, *, tm):
    b, s, _ = x.shape
    tile = pl.BlockSpec((None, tm, D_MODEL), lambda i, m: (i, m, 0))
    est = 2 * 4 * tm * D_MODEL * 4 + 2 * D_MODEL * D_MODEL * 2 + 6 * tm * D_MODEL * 4
    return pl.pallas_call(
        _mlstm_out_kernel,
        grid=(b, s // tm),
        in_specs=[
            tile, tile, tile,
            pl.BlockSpec((None, N_MOD, D_MODEL), lambda i, m: (i, 0, 0)),
            pl.BlockSpec((1, D_MODEL), lambda i, m: (0, 0)),
            _resident(wo.shape),
            pl.BlockSpec((1, V_A), lambda i, m: (0, 0)),
            _resident(wout.shape),
        ],
        out_specs=tile,
        out_shape=jax.ShapeDtypeStruct(x.shape, F32),
        compiler_params=_params(("parallel", "parallel"), est),
        name="mlstm_out",
    )(x, hf, hb, mod, nw, wo, mnw, wout)


def _attn_in_kernel(x_ref, mod_ref, nw_ref, wq_ref, wkt_ref, wv_ref, seg_ref,
                    cq_ref, sq_ref, ckt_ref, skt_ref, qw_ref, qwsw_ref, kw_ref, kwsw_ref,
                    q_ref, kt_ref, v_ref, *, q_scale, rot_half):
    h = _norm_mod(x_ref[...], nw_ref[...], mod_ref[3:4, :], mod_ref[4:5, :]).astype(BF16)
    q = _dot(h, wq_ref[...])
    v = _dot(h, wv_ref[...])
    kt = _dot_nt(wkt_ref[...], h)
    ssq = _dot((q * q).astype(BF16), seg_ref[...])

    lane_v = lax.broadcasted_iota(jnp.int32, v.shape, 1)
    v_ref[...] = jnp.where(lane_v % V7X_LANES == HEAD_DIM, 1.0, v).astype(BF16)

    cos_k = ckt_ref[...] * kw_ref[...]
    sin_k = skt_ref[...] * kwsw_ref[...]
    outs = []
    for j in range(ATT_KV_HEADS):
        a = kt[j * HEAD_DIM:(j + 1) * HEAD_DIM, :]
        partner = jnp.concatenate(
            [a[blk + off:blk + off + rot_half, :] for blk in range(0, HEAD_DIM, 2 * rot_half)
             for off in (rot_half, 0)], axis=0)
        rk = lax.rsqrt(jnp.mean(a * a, axis=0, keepdims=True) + EPS)
        outs.append(rk * (a * cos_k + partner * sin_k))
    kt_ref[...] = jnp.concatenate(outs, axis=0).astype(BF16)

    rq = lax.rsqrt(ssq * (1.0 / HEAD_DIM) + EPS)
    cos_q = cq_ref[...] * qw_ref[...]
    sin_q = sq_ref[...] * qwsw_ref[...]
    lane = lax.broadcasted_iota(jnp.int32, cos_q.shape, 1)
    first = lane % (2 * rot_half) < rot_half
    for c in range(Q_ATT // V7X_LANES):
        cols = slice(c * V7X_LANES, (c + 1) * V7X_LANES)
        blk = q[:, cols]
        partner = jnp.where(first, pltpu.roll(blk, V7X_LANES - rot_half, axis=1),
                            pltpu.roll(blk, rot_half, axis=1))
        q_ref[:, cols] = ((rq[:, cols] * q_scale) * (blk * cos_q + partner * sin_q)).astype(BF16)


def _attn_in(x, mod, nw, w, tabs, *, tm, tk, q_scale, rot_half):
    b, s, _ = x.shape
    per = tk // tm
    const2 = lambda i, m: (0, 0)
    est = 2 * tm * D_MODEL * 4 + (2 * D_MODEL * Q_ATT + 3 * D_MODEL * KV_ATT) * 2 \
        + 2 * tm * (Q_ATT + 2 * KV_ATT) * 2 + 8 * tm * Q_ATT * 4 + 8 * tm * V7X_LANES * 4
    return pl.pallas_call(
        functools.partial(_attn_in_kernel, q_scale=q_scale, rot_half=rot_half),
        grid=(b, s // tm),
        in_specs=[
            pl.BlockSpec((None, tm, D_MODEL), lambda i, m: (i, m, 0)),
            pl.BlockSpec((None, N_MOD, D_MODEL), lambda i, m: (i, 0, 0)),
            pl.BlockSpec((1, D_MODEL), const2),
            _resident(w["wq"].shape), _resident(w["wkt"].shape),
            _resident(w["wv"].shape), _resident(w["seg"].shape),
            pl.BlockSpec((tm, V7X_LANES), lambda i, m: (m, 0)),
            pl.BlockSpec((tm, V7X_LANES), lambda i, m: (m, 0)),
            pl.BlockSpec((HEAD_DIM, tm), lambda i, m: (0, m)),
            pl.BlockSpec((HEAD_DIM, tm), lambda i, m: (0, m)),
            pl.BlockSpec((1, V7X_LANES), const2), pl.BlockSpec((1, V7X_LANES), const2),
            pl.BlockSpec((HEAD_DIM, 1), const2), pl.BlockSpec((HEAD_DIM, 1), const2),
        ],
        out_specs=[
            pl.BlockSpec((None, tm, Q_ATT), lambda i, m: (i, m, 0)),
            pl.BlockSpec((None, None, KV_ATT, tm), lambda i, m: (i, m // per, 0, m % per)),
            pl.BlockSpec((None, tm, VX_ATT), lambda i, m: (i, m, 0)),
        ],
        out_shape=[
            jax.ShapeDtypeStruct((b, s, Q_ATT), BF16),
            jax.ShapeDtypeStruct((b, s // tk, KV_ATT, tk), BF16),
            jax.ShapeDtypeStruct((b, s, VX_ATT), BF16),
        ],
        compiler_params=_params(("parallel", "parallel"), est),
        name="attn_in",
    )(x, mod, nw, w["wq"], w["wkt"], w["wv"], w["seg"],
      tabs["cos_q"], tabs["sin_q"], tabs["cos_kt"], tabs["sin_kt"],
      w["qw"], w["qwsw"], w["kw"], w["kwsw"])


def _swa_kernel(bound_ref, sink_ref, q_ref, ktl_ref, ktm_ref, ktr_ref, vl_ref, vm_ref, vr_ref, o_ref,
                *, seq, tq, bounded):
    nsub = tq // BLOCK
    j = pl.program_id(1)
    kt_win = jnp.concatenate([ktl_ref[...], ktm_ref[...], ktr_ref[...]], axis=1)
    v_win = jnp.concatenate([vl_ref[...], vm_ref[...], vr_ref[...]], axis=0)
    qi = lax.broadcasted_iota(jnp.int32, (BLOCK, 3 * BLOCK), 0)
    kj = lax.broadcasted_iota(jnp.int32, (BLOCK, 3 * BLOCK), 1) - BLOCK
    band_bias = jnp.where(jnp.abs(qi - kj) <= WINDOW, 0.0, -jnp.inf)
    biases = []
    for i in range(nsub):
        bias = band_bias
        if i == 0:
            bias = jnp.where(kj + j * tq >= 0, bias, -jnp.inf)
        if i == nsub - 1:
            bias = jnp.where(kj + (j * tq + i * BLOCK) < seq, bias, -jnp.inf)
        biases.append(bias)
    q_tiles = [q_ref[i * BLOCK:(i + 1) * BLOCK, :] for i in range(nsub)]

    per_blk = ATT_HEADS // SWA_STAGE_HEADS
    n_stage = nsub * per_blk

    def heads_of(n):
        i, part = divmod(n, per_blk)
        return i, range(part * SWA_STAGE_HEADS, (part + 1) * SWA_STAGE_HEADS)

    def scores(n):
        i, heads = heads_of(n)
        return [_dot(q_tiles[i][:, h * HEAD_DIM:(h + 1) * HEAD_DIM],
                     kt_win[(h // ATT_GROUP) * HEAD_DIM:(h // ATT_GROUP + 1) * HEAD_DIM, i * BLOCK:(i + 3) * BLOCK])
                + biases[i] for h in heads]

    pending = {n: scores(n) for n in range(min(SWA_AHEAD, n_stage))}
    outs = []
    for n in range(n_stage):
        if n + SWA_AHEAD < n_stage:
            pending[n + SWA_AHEAD] = scores(n + SWA_AHEAD)
        s_heads = pending.pop(n)
        i, heads = heads_of(n)
        sinks = [sink_ref[h] * LOG2E for h in heads]
        if bounded:
            shifts = [jnp.maximum(bound_ref[0], sk) for sk in sinks]
            p = jnp.concatenate([jnp.exp2(s - m) for s, m in zip(s_heads, shifts)], axis=0).astype(BF16)
            sink_w = jnp.concatenate([jnp.full((BLOCK, 1), jnp.exp2(sk - m), F32) for sk, m in zip(sinks, shifts)],
                                     axis=0)
        else:
            s = jnp.concatenate(s_heads, axis=0)
            sink = jnp.concatenate([jnp.full((BLOCK, 1), sk, F32) for sk in sinks], axis=0)
            m = jnp.maximum(jnp.max(s, axis=1, keepdims=True), sink)
            p = jnp.exp2(s - m).astype(BF16)
            sink_w = jnp.exp2(sink - m)
        r = jnp.concatenate(
            [_dot(p[a * BLOCK:(a + 1) * BLOCK, :],
                  v_win[i * BLOCK:(i + 3) * BLOCK, (h // ATT_GROUP) * V7X_LANES:(h // ATT_GROUP + 1) * V7X_LANES])
             for a, h in enumerate(heads)], axis=0)
        o = r[:, :HEAD_DIM] / (r[:, HEAD_DIM:HEAD_DIM + 1] + sink_w)
        outs += [o[a * BLOCK:(a + 1) * BLOCK, :] for a in range(SWA_STAGE_HEADS)]
        if len(outs) == ATT_HEADS:
            o_ref[i * BLOCK:(i + 1) * BLOCK, :] = jnp.concatenate(outs, axis=1).astype(BF16)
            outs = []


def _swa(bound, q, kt, v, sink, *, tq, bounded):
    b, s, _ = q.shape
    nsub = tq // BLOCK
    nblk = s // BLOCK
    nq = s // tq
    left = lambda j: jnp.maximum(j * nsub - 1, 0)
    right = lambda j: jnp.minimum((j + 1) * nsub, nblk - 1)
    est = 2 * (tq * Q_ATT * 2 * 2 + (tq + 2 * BLOCK) * (KV_ATT + VX_ATT) * 2) + (12 << 20)
    return pl.pallas_call(
        functools.partial(_swa_kernel, seq=s, tq=tq, bounded=bounded),
        grid=(b, nq),
        in_specs=[
            pl.BlockSpec(memory_space=pltpu.SMEM),
            pl.BlockSpec(memory_space=pltpu.SMEM),
            pl.BlockSpec((None, tq, Q_ATT), lambda i, j: (i, j, 0)),
            pl.BlockSpec((None, None, KV_ATT, BLOCK), lambda i, j: (i, 0, 0, left(j))),
            pl.BlockSpec((None, None, KV_ATT, tq), lambda i, j: (i, 0, 0, j)),
            pl.BlockSpec((None, None, KV_ATT, BLOCK), lambda i, j: (i, 0, 0, right(j))),
            pl.BlockSpec((None, BLOCK, VX_ATT), lambda i, j: (i, left(j), 0)),
            pl.BlockSpec((None, tq, VX_ATT), lambda i, j: (i, j, 0)),
            pl.BlockSpec((None, BLOCK, VX_ATT), lambda i, j: (i, right(j), 0)),
        ],
        out_specs=pl.BlockSpec((None, tq, Q_ATT), lambda i, j: (i, j, 0)),
        out_shape=jax.ShapeDtypeStruct((b, s, Q_ATT), BF16),
        compiler_params=_params(("parallel", "parallel"), est),
        name="swa_bounded" if bounded else "swa",
    )(bound, sink, q, kt, kt, kt, v, v, v)


def _axial_kernel(bound_ref, q_ref, kt_ref, v_ref, o_ref, m_ref, acc_ref, s_ref, *, tk, nk, bounded):
    q_rows = q_ref[...]
    qs = [q_rows[:, h * HEAD_DIM:(h + 1) * HEAD_DIM] for h in range(ATT_HEADS)]
    m_ref[...] = jnp.full(m_ref.shape, -jnp.inf, F32)
    acc_ref[...] = jnp.zeros_like(acc_ref)

    def scores(h, kt_c):
        g = h // ATT_GROUP
        return _dot(qs[h], kt_c[g * HEAD_DIM:(g + 1) * HEAD_DIM, :])

    kt_0 = kt_ref[0]
    for h in range(AXIAL_AHEAD):
        s_ref[h] = scores(h, kt_0)

    def body(c, carry):
        kt_c = kt_ref[c]
        kt_n = kt_ref[jnp.minimum(c + 1, nk - 1)]
        v_c = v_ref[pl.ds(pl.multiple_of(c * tk, tk), tk), :]
        pending = {h: s_ref[h] for h in range(AXIAL_AHEAD)}
        for h in range(ATT_HEADS):
            ahead = h + AXIAL_AHEAD
            if ahead < ATT_HEADS:
                pending[ahead] = scores(ahead, kt_c)
            else:
                s_ref[ahead - ATT_HEADS] = scores(ahead - ATT_HEADS, kt_n)
            g = h // ATT_GROUP
            s = pending.pop(h)
            v_ext = v_c[:, g * V7X_LANES:(g + 1) * V7X_LANES]
            if bounded:
                acc_ref[h] += _dot(jnp.exp2(s - bound_ref[0]).astype(BF16), v_ext)
            else:
                m_prev = m_ref[h]
                m_new = jnp.maximum(m_prev, jnp.max(s, axis=1, keepdims=True))
                p = jnp.exp2(s - m_new).astype(BF16)
                acc_ref[h] = jnp.exp2(m_prev - m_new) * acc_ref[h] + _dot(p, v_ext)
                m_ref[h] = m_new
        return carry

    lax.fori_loop(0, nk, body, 0, unroll=min(nk, AXIAL_UNROLL))
    outs = []
    for h in range(ATT_HEADS):
        acc = acc_ref[h]
        outs.append(acc[:, :HEAD_DIM] / acc[:, HEAD_DIM:HEAD_DIM + 1])
    o_ref[...] = jnp.concatenate(outs, axis=1).astype(BF16)


def _axial(bound, q, kt, v, *, tq, tk, bounded):
    b, s, _ = q.shape
    nk = s // tk
    rows = ATT_GROUP * tq
    est = 2 * tq * Q_ATT * 2 * 2 + s * (KV_ATT + VX_ATT) * 2 \
        + ATT_KV_HEADS * (2 * rows * V7X_LANES * 4 + rows * tk * 6) + (4 << 20)
    return pl.pallas_call(
        functools.partial(_axial_kernel, tk=tk, nk=nk, bounded=bounded),
        grid=(b, s // tq),
        in_specs=[
            pl.BlockSpec(memory_space=pltpu.SMEM),
            pl.BlockSpec((None, tq, Q_ATT), lambda i, j: (i, j, 0)),
            pl.BlockSpec((None, nk, KV_ATT, tk), lambda i, j: (i, 0, 0, 0), pipeline_mode=pl.Buffered(1)),
            pl.BlockSpec((None, s, VX_ATT), lambda i, j: (i, 0, 0), pipeline_mode=pl.Buffered(1)),
        ],
        out_specs=pl.BlockSpec((None, tq, Q_ATT), lambda i, j: (i, j, 0)),
        out_shape=jax.ShapeDtypeStruct((b, s, Q_ATT), BF16),
        scratch_shapes=[
            pltpu.VMEM((ATT_HEADS, tq, 1), F32), pltpu.VMEM((ATT_HEADS, tq, V7X_LANES), F32),
            pltpu.VMEM((AXIAL_AHEAD, tq, tk), F32),
        ],
        compiler_params=_params(("parallel", "parallel"), est),
        name="axial_bounded" if bounded else "axial",
    )(bound, q, kt, v)


def _rope_tables(pos, dim):
    inv = ROPE_THETA ** (-jnp.arange(0, dim, 2, dtype=F32) / dim)
    ang = pos.astype(F32)[:, None] * inv[None, :]
    ang = jnp.concatenate([ang, ang], axis=-1)
    return jnp.cos(ang), jnp.sin(ang)


def _rot_half_perm(widths):
    perm, sign, base = [], [], 0
    for w in widths:
        half = w // 2
        perm += [base + half + i for i in range(half)] + [base + i for i in range(half)]
        sign += [-1.0] * half + [1.0] * half
        base += w
    return jnp.array(perm, jnp.int32), jnp.array(sign, F32)


def _attn_tables(cos, sin):
    reps = V7X_LANES // HEAD_DIM
    return {
        "cos_q": jnp.tile(cos, (1, reps)), "sin_q": jnp.tile(sin, (1, reps)),
        "cos_kt": cos.T, "sin_kt": sin.T,
    }


def _attn_weights(w_in, q_norm, k_norm, widths):
    perm, sign = _rot_half_perm(widths)
    wq = w_in[:, :Q_ATT]
    wk = w_in[:, Q_ATT:Q_ATT + KV_ATT]
    wv = w_in[:, Q_ATT + KV_ATT:]
    seg_id = jnp.arange(Q_ATT) // HEAD_DIM
    reps = V7X_LANES // HEAD_DIM
    return {
        "wq": wq.astype(BF16), "wkt": wk.T.astype(BF16),
        "wv": jnp.pad(wv.reshape(D_MODEL, ATT_KV_HEADS, HEAD_DIM),
                      ((0, 0), (0, 0), (0, V7X_LANES - HEAD_DIM))).reshape(D_MODEL, VX_ATT).astype(BF16),
        "seg": (seg_id[:, None] == seg_id[None, :]).astype(BF16),
        "qw": jnp.tile(q_norm, reps)[None, :], "qwsw": jnp.tile(q_norm[perm] * sign, reps)[None, :],
        "kw": k_norm[:, None], "kwsw": (k_norm[perm] * sign)[:, None],
        "bound": (BOUND_SLACK * LOG2E * HEAD_DIM ** 0.5
                  * jnp.max(jnp.abs(q_norm)) * jnp.max(jnp.abs(k_norm))).reshape(1).astype(F32),
    }


def _ffn_weights(w13, w2):
    return w13.astype(BF16), w2.astype(BF16)


def _mlstm_weights(w_in, b_gate, norm_w, w_out):
    wq = w_in[:, :QK_A]
    wk = w_in[:, QK_A:2 * QK_A]
    wv = w_in[:, 2 * QK_A:2 * QK_A + V_A]
    wo = w_in[:, 2 * QK_A + V_A:2 * QK_A + 2 * V_A]
    wg = w_in[:, 2 * QK_A + 2 * V_A:]
    return {
        "wq": wq.astype(BF16), "wkt": wk.T.astype(BF16), "wv": wv.astype(BF16), "wo": wo.astype(BF16),
        "wgt": wg.T.astype(BF16), "bgt": b_gate[:, None],
        "mnw": norm_w[None, :], "wout": w_out.astype(BF16),
    }


def _tile(s, want):
    t = min(s, want)
    assert s % t == 0
    return t


def _bounded_or_exact(attend, bound, *operands):
    return lax.cond(bound[0] <= SCORE_BOUND_MAX,
                    functools.partial(attend, bounded=True), functools.partial(attend, bounded=False),
                    bound, *operands)


def _trunk(x, mod_all, prep):
    b, s, _ = x.shape
    tm_ffn = _tile(s, 1024)
    tm = _tile(s, 512)
    for i in range(DEPTH):
        mod = mod_all[i].reshape(b, N_MOD, D_MODEL)
        lw = prep["layers"][i]
        x = _ffn(x, mod, lw["nw"][0:1], *lw["ffn0"], row0=0, tm=tm_ffn)
        kind = i % N_MIXERS
        mw = lw["mixer"]
        if kind == 0:
            q, kt, v, rows, cols = _mlstm_in(x, mod, lw["nw"][1:2], mw["wq"], mw["wkt"], mw["wv"],
                                             mw["wgt"], mw["bgt"], tm=tm_ffn)
            hf, hb = _mlstm_scan(q, kt, v, rows, cols, tb=_tile(s, 1024))
            x = _mlstm_out(x, hf, hb, mod, lw["nw"][1:2], mw["wo"], mw["mnw"], mw["wout"], tm=tm_ffn)
        elif kind == 1:
            q, kt, v = _attn_in(x, mod, lw["nw"][1:2], mw, prep["swa_tabs"][s], tm=tm, tk=s,
                                q_scale=HEAD_DIM ** -0.5 * LOG2E, rot_half=HEAD_DIM // 2)
            attn = _bounded_or_exact(functools.partial(_swa, tq=_tile(s, 512)), mw["bound"], q, kt, v, mw["sink"])
        else:
            tk = _tile(s, 1024)
            q, kt, v = _attn_in(x, mod, lw["nw"][1:2], mw, prep["axial_tabs"][s], tm=tm, tk=tk,
                                q_scale=HEAD_DIM ** -0.5 * LOG2E, rot_half=HEAD_DIM // 4)
            attn = _bounded_or_exact(functools.partial(_axial, tq=_tile(s, 128), tk=tk), mw["bound"], q, kt, v)
        if kind == 0:
            x = _ffn(x, mod, lw["nw"][2:3], *lw["ffn1"], row0=6, tm=tm_ffn)
        else:
            x = _ffn(x, mod, lw["nw"][2:3], *lw["ffn1"], row0=6, tm=tm_ffn, attn=attn, wout=mw["wout"])
    return x


def _prepare(seqs, ffn_w13, ffn_w2, norm_w,
             mlstm_w_in, mlstm_b_gate, mlstm_norm_w, mlstm_w_out,
             swa_w_in, swa_q_norm, swa_k_norm, swa_sink, swa_w_out,
             axial_w_in, axial_q_norm, axial_k_norm, axial_w_out):
    layers = []
    for i in range(DEPTH):
        kind, j = i % N_MIXERS, i // N_MIXERS
        if kind == 0:
            mixer = _mlstm_weights(mlstm_w_in[j], mlstm_b_gate[j], mlstm_norm_w[j], mlstm_w_out[j])
        elif kind == 1:
            mixer = _attn_weights(swa_w_in[j], swa_q_norm[j], swa_k_norm[j], (HEAD_DIM,))
            mixer["sink"] = swa_sink[j]
            mixer["wout"] = swa_w_out[j].astype(BF16)
        else:
            mixer = _attn_weights(axial_w_in[j], axial_q_norm[j], axial_k_norm[j], (HEAD_DIM // 2, HEAD_DIM // 2))
            mixer["wout"] = axial_w_out[j].astype(BF16)
        layers.append({
            "nw": norm_w[i],
            "ffn0": _ffn_weights(ffn_w13[i, 0], ffn_w2[i, 0]),
            "ffn1": _ffn_weights(ffn_w13[i, 1], ffn_w2[i, 1]),
            "mixer": mixer,
        })
    swa_tabs, axial_tabs = {}, {}
    for s in seqs:
        swa_tabs[s] = _attn_tables(*_rope_tables(jnp.arange(s), HEAD_DIM))
        rows = s // GRID_W
        row_ids = jnp.repeat(jnp.arange(rows), GRID_W)
        col_ids = jnp.tile(jnp.arange(GRID_W), rows)
        rc, rs = _rope_tables(row_ids, HEAD_DIM // 2)
        cc, cs = _rope_tables(col_ids, HEAD_DIM // 2)
        axial_tabs[s] = _attn_tables(jnp.concatenate([rc, cc], axis=-1), jnp.concatenate([rs, cs], axis=-1))
    return {"layers": layers, "swa_tabs": swa_tabs, "axial_tabs": axial_tabs}


def kernel(x_prompt, x_sample, c_prompt, c_sample, ffn_w13, ffn_w2, ada_w, ada_b, norm_w, mlstm_w_in, mlstm_b_gate, mlstm_norm_w, mlstm_w_out, swa_w_in, swa_q_norm, swa_k_norm, swa_sink, swa_w_out, axial_w_in, axial_q_norm, axial_k_norm, axial_w_out):
    prep = _prepare({x_prompt.shape[1], x_sample.shape[1]}, ffn_w13, ffn_w2, norm_w,
                    mlstm_w_in, mlstm_b_gate, mlstm_norm_w, mlstm_w_out,
                    swa_w_in, swa_q_norm, swa_k_norm, swa_sink, swa_w_out,
                    axial_w_in, axial_q_norm, axial_k_norm, axial_w_out)
    nb_prompt = c_prompt.shape[0]
    mod_all = _ada_mod(jnp.concatenate([c_prompt, c_sample], axis=0), ada_w, ada_b)
    y_prompt = _trunk(x_prompt, mod_all[:, :nb_prompt], prep)
    y_sample = _trunk(x_sample, mod_all[:, nb_prompt:], prep)
    return (y_prompt, y_sample)
```

```python
import functools

import jax
import jax.numpy as jnp
from jax import lax
from jax.experimental import pallas as pl
from jax.experimental.pallas import tpu as pltpu

F32 = jnp.float32
BF16 = jnp.bfloat16

D_MODEL = 1024
DEPTH = 4
N_MIXERS = 3
D_FF = 2816
EPS = 1e-6
N_MOD = 9
A_HEADS = 8
A_DK = 64
A_DV = 128
A_CHUNK = 128
A_GATE_CAP = 15.0
ATT_HEADS = 16
ATT_KV_HEADS = 4
ATT_GROUP = ATT_HEADS // ATT_KV_HEADS
HEAD_DIM = 64
WINDOW = 128
BLOCK = 128
ROPE_THETA = 10000.0
GRID_W = 64

V7X_VMEM_BYTES = 64 * 1024 * 1024
V7X_LANES = 128
V7X_MXU_DIM = 256

FFN_CHUNK = V7X_MXU_DIM
N_FFN_CHUNKS = D_FF // FFN_CHUNK
QK_A = A_HEADS * A_DK
V_A = A_HEADS * A_DV
Q_ATT = ATT_HEADS * HEAD_DIM
KV_ATT = ATT_KV_HEADS * HEAD_DIM
VX_ATT = ATT_KV_HEADS * V7X_LANES
LOG2E = 1.4426950408889634
AXIAL_AHEAD = 3
AXIAL_UNROLL = 4
SWA_AHEAD = 2
BOUND_SLACK = 1.02
SCORE_BOUND_MAX = 48.0
SWA_STAGE_HEADS = 2
MLSTM_OUT_SLABS = 2

NT_DIMS = (((1,), (1,)), ((), ()))


def _vmem_limit(est_bytes):
    return int(min(est_bytes * 5 // 4 + (4 << 20), V7X_VMEM_BYTES - (4 << 20)))


def _params(sem, est_bytes):
    return pltpu.CompilerParams(dimension_semantics=sem, vmem_limit_bytes=_vmem_limit(est_bytes))


def _resident(shape):
    nd = len(shape)
    return pl.BlockSpec(shape, lambda *_: (0,) * nd, pipeline_mode=pl.Buffered(1))


def _dot(a, b):
    return jnp.dot(a, b, preferred_element_type=F32)


def _dot_nt(a, b):
    return lax.dot_general(a, b, NT_DIMS, preferred_element_type=F32)


def _norm_mod(x, nw, shift, scale):
    ms = jnp.mean(x * x, axis=-1, keepdims=True)
    y = x * lax.rsqrt(ms + EPS)
    return (y * nw) * (1.0 + scale) + shift


def _sigmoid(x):
    return 1.0 / (1.0 + jnp.exp(-x))


def _ada_kernel(c_ref, w_ref, b_ref, o_ref):
    c = c_ref[...]
    a = c * _sigmoid(c)
    o_ref[...] = _dot(a, w_ref[...]) + b_ref[...]


def _ada_mod(c_all, ada_w, ada_b):
    bc = c_all.shape[0]
    n_out = N_MOD * D_MODEL
    tn = D_MODEL
    return pl.pallas_call(
        _ada_kernel,
        grid=(DEPTH, n_out // tn),
        in_specs=[
            pl.BlockSpec((bc, D_MODEL), lambda i, n: (0, 0)),
            pl.BlockSpec((None, D_MODEL, tn), lambda i, n: (i, 0, n)),
            pl.BlockSpec((None, 1, tn), lambda i, n: (i, 0, n)),
        ],
        out_specs=pl.BlockSpec((None, bc, tn), lambda i, n: (i, 0, n)),
        out_shape=jax.ShapeDtypeStruct((DEPTH, bc, n_out), F32),
        compiler_params=_params(("parallel", "parallel"), 4 * (2 * D_MODEL * tn + 4 * bc * tn)),
        name="ada_mod",
    )(c_all, ada_w, ada_b.reshape(DEPTH, 1, n_out))


def _ffn_kernel(*refs, row0, mixer_out):
    if mixer_out:
        x_ref, attn_ref, wout_ref, mod_ref, nw_ref, w13_ref, w2_ref, o_ref, h_ref, a_ref = refs
        o_ref[...] = x_ref[...] + mod_ref[5:6, :] * _dot(attn_ref[...], wout_ref[...])
        x_ref = o_ref
    else:
        x_ref, mod_ref, nw_ref, w13_ref, w2_ref, o_ref, h_ref, a_ref = refs
    shift = mod_ref[row0:row0 + 1, :]
    scale = mod_ref[row0 + 1:row0 + 2, :]
    gate = mod_ref[row0 + 2:row0 + 3, :]
    h_ref[...] = _norm_mod(x_ref[...], nw_ref[...], shift, scale).astype(BF16)

    for c in range(N_FFN_CHUNKS):
        cols = slice(c * FFN_CHUNK, (c + 1) * FFN_CHUNK)
        h = h_ref[...]
        g = _dot(h, w13_ref[:, cols])
        u = _dot(h, w13_ref[:, D_FF + c * FFN_CHUNK:D_FF + (c + 1) * FFN_CHUNK])
        a_ref[:, cols] = ((g * _sigmoid(g)) * u).astype(BF16)
    o_ref[...] = x_ref[...] + (0.5 * gate) * _dot(a_ref[...], w2_ref[...])


def _ffn(x, mod, nw, w13, w2, *, layer, sub, row0, tm, attn=None, wout=None):
    b, s, _ = x.shape

    def stacked(shape):
        return pl.BlockSpec((None, None) + shape, lambda i, m: (layer, sub, 0, 0), pipeline_mode=pl.Buffered(1))
    tile = pl.BlockSpec((None, tm, D_MODEL), lambda i, m: (i, m, 0))
    mixer_out = attn is not None
    est = 4 * tm * D_MODEL * 4 + 3 * D_MODEL * D_FF * 2 + tm * (D_MODEL + D_FF) * 2 + 4 * tm * FFN_CHUNK * 4 \
        + tm * D_MODEL * 4
    mixer_specs, mixer_args = [], []
    if mixer_out:
        est += 2 * tm * D_MODEL * 2 + D_MODEL * D_MODEL * 2
        mixer_specs, mixer_args = [tile, _resident(wout.shape)], [attn, wout]
    return pl.pallas_call(
        functools.partial(_ffn_kernel, row0=row0, mixer_out=mixer_out),
        grid=(b, s // tm),
        in_specs=[tile] + mixer_specs + [
            pl.BlockSpec((None, N_MOD, D_MODEL), lambda i, m: (i, 0, 0)),
            pl.BlockSpec((1, D_MODEL), lambda i, m: (0, 0)),
            stacked(w13.shape[2:]),
            stacked(w2.shape[2:]),
        ],
        out_specs=tile,
        out_shape=jax.ShapeDtypeStruct(x.shape, F32),
        scratch_shapes=[pltpu.VMEM((tm, D_MODEL), BF16), pltpu.VMEM((tm, D_FF), BF16)],
        compiler_params=_params(("parallel", "parallel"), est),
        name="ffn_mixer_out" if mixer_out else "ffn",
    )(x, *mixer_args, mod, nw, w13, w2)


def _log_sigmoid(x):
    return jnp.minimum(x, 0.0) - jnp.log1p(jnp.exp(-jnp.abs(x)))


def _gate_act(g, is_forget):
    g = A_GATE_CAP * jnp.tanh(g / A_GATE_CAP)
    return jnp.where(is_forget, _log_sigmoid(g), g)


def _mlstm_in_kernel(x_ref, mod_ref, nw_ref, wq_ref, wkt_ref, wv_ref, wgt_ref, bgt_ref,
                     q_ref, kt_ref, v_ref, rows_ref, cols_ref):
    L, H = A_CHUNK, A_HEADS
    cpt = kt_ref.shape[0]
    h = _norm_mod(x_ref[...], nw_ref[...], mod_ref[3:4, :], mod_ref[4:5, :]).astype(BF16)
    g_row = _dot_nt(wgt_ref[...], h) + bgt_ref[...]
    q_ref[...] = (_dot(h, wq_ref[...]) * (A_DK ** -0.5)).astype(BF16)

    row_id = lax.broadcasted_iota(jnp.int32, g_row.shape, 0)
    g_row = _gate_act(g_row, (row_id // H) % 2 == 1) * LOG2E
    ri = lax.broadcasted_iota(jnp.int32, (L, L), 0)
    ci = lax.broadcasted_iota(jnp.int32, (L, L), 1)
    tris = (jnp.where(ri <= ci, 1.0, 0.0).astype(BF16), jnp.where(ri >= ci, 1.0, 0.0).astype(BF16))
    eye = jnp.where(ri == ci, 1.0, 0.0).astype(BF16)
    lane = lax.broadcasted_iota(jnp.int32, (cpt * H, L), 1)

    def chunk_rows(first_row):
        return jnp.concatenate([g_row[first_row:first_row + H, j * L:(j + 1) * L] for j in range(cpt)], axis=0)

    igs = [chunk_rows(2 * d * H) for d in range(2)]
    bs = [_cumsum_rows(chunk_rows((2 * d + 1) * H), tris[d]) for d in range(2)]
    v_ref[...] = _dot(h, wv_ref[...]).astype(BF16)

    col_parts = []
    for d in range(2):
        b, ig = bs[d], igs[d]
        z = b - ig
        cm = _cummax_lanes(-z, lane, reverse=(d == 1))
        b_last = b[:, L - 1:L] if d == 0 else b[:, 0:1]
        log_w = (b_last - b) + ig
        lw_max = jnp.broadcast_to(jnp.max(log_w, axis=1, keepdims=True), b.shape)
        b_last = jnp.broadcast_to(b_last, b.shape)
        for j in range(cpt):
            rows_ref[j, d] = jnp.concatenate([a[j * H:(j + 1) * H, :] for a in (z, log_w, lw_max, b_last)], axis=0)
        col_parts.append((cm, b))
    kt = _dot_nt(wkt_ref[...], h).astype(BF16)
    for j in range(cpt):
        kt_ref[j] = kt[:, j * L:(j + 1) * L]
        packed = jnp.concatenate([a[j * H:(j + 1) * H, :] for d in range(2) for a in col_parts[d]]
                                 + [jnp.zeros((L - 4 * H, L), F32)], axis=0)
        cols_ref[j] = _transpose_rows(eye, packed)


def _mlstm_in(x, mod, nw, wq, wkt, wv, wgt, bgt, *, tm):
    b, s, _ = x.shape
    nch, cpt = s // A_CHUNK, tm // A_CHUNK
    ng = 4 * A_HEADS
    est = 2 * tm * D_MODEL * 4 + 2 * (D_MODEL * (2 * QK_A + V_A + V7X_LANES)) * 2 \
        + 2 * tm * (QK_A * 2 * 2 + V_A * 2 + 2 * V7X_LANES * 4) + tm * (2 * QK_A + V_A) * 4 + (4 << 20)
    return pl.pallas_call(
        _mlstm_in_kernel,
        grid=(b, s // tm),
        in_specs=[
            pl.BlockSpec((None, tm, D_MODEL), lambda i, m: (i, m, 0)),
            pl.BlockSpec((None, N_MOD, D_MODEL), lambda i, m: (i, 0, 0)),
            pl.BlockSpec((1, D_MODEL), lambda i, m: (0, 0)),
            _resident(wq.shape), _resident(wkt.shape), _resident(wv.shape),
            _resident(wgt.shape), _resident(bgt.shape),
        ],
        out_specs=[
            pl.BlockSpec((None, tm, QK_A), lambda i, m: (i, m, 0)),
            pl.BlockSpec((None, cpt, QK_A, A_CHUNK), lambda i, m: (i, m, 0, 0)),
            pl.BlockSpec((None, tm, V_A), lambda i, m: (i, m, 0)),
            pl.BlockSpec((None, cpt, 2, ng, A_CHUNK), lambda i, m: (i, m, 0, 0, 0)),
            pl.BlockSpec((None, cpt, A_CHUNK, A_CHUNK), lambda i, m: (i, m, 0, 0)),
        ],
        out_shape=[
            jax.ShapeDtypeStruct((b, s, QK_A), BF16),
            jax.ShapeDtypeStruct((b, nch, QK_A, A_CHUNK), BF16),
            jax.ShapeDtypeStruct((b, s, V_A), BF16),
            jax.ShapeDtypeStruct((b, nch, 2, ng, A_CHUNK), F32),
            jax.ShapeDtypeStruct((b, nch, A_CHUNK, A_CHUNK), F32),
        ],
        compiler_params=_params(("parallel", "parallel"), est),
        name="mlstm_in",
    )(x, mod, nw, wq, wkt, wv, wgt, bgt)


def _split3(a):
    hi = a.astype(BF16)
    r1 = a - hi.astype(F32)
    mid = r1.astype(BF16)
    lo = (r1 - mid.astype(F32)).astype(BF16)
    return hi, mid, lo


def _cumsum_rows(a, tri):
    hi, mid, lo = _split3(a)
    return _dot(hi, tri) + _dot(mid, tri) + _dot(lo, tri)


def _transpose_rows(eye, a):
    hi, mid, lo = _split3(a)
    return _dot_nt(eye, hi) + _dot_nt(eye, mid) + _dot_nt(eye, lo)


def _cummax_lanes(x, lane, reverse):
    n = x.shape[1]
    k = 1
    while k < n:
        if reverse:
            shifted = jnp.where(lane < n - k, pltpu.roll(x, n - k, axis=1), -jnp.inf)
        else:
            shifted = jnp.where(lane >= k, pltpu.roll(x, k, axis=1), -jnp.inf)
        x = jnp.maximum(x, shifted)
        k *= 2
    return x


def _mlstm_scan_kernel(qf_ref, ktf_ref, vf_ref, rowsf_ref, colsf_ref, qb_ref, ktb_ref, vb_ref, rowsb_ref, colsb_ref,
                       hf_ref, hb_ref, c_ref, m_ref, *, cps):
    L = A_CHUNK
    H = A_HEADS

    @pl.when(pl.program_id(1) == 0)
    def _():
        c_ref[...] = jnp.zeros_like(c_ref)
        m_ref[...] = jnp.full(m_ref.shape, -jnp.inf, F32)

    ri = lax.broadcasted_iota(jnp.int32, (L, L), 0)
    ci = lax.broadcasted_iota(jnp.int32, (L, L), 1)
    keeps = (ri >= ci, ri <= ci)
    ones_blk = jnp.ones((L, L), BF16)
    refs = ((qf_ref, ktf_ref, vf_ref, rowsf_ref, colsf_ref, hf_ref),
            (qb_ref, ktb_ref, vb_ref, rowsb_ref, colsb_ref, hb_ref))

    def body(c, carry):
        stages = []
        for d in range(2):
            q_ref, kt_ref, v_ref, rows_ref, cols_ref, out_ref = refs[d]
            cc = c if d == 0 else cps - 1 - c
            r0 = pl.multiple_of(cc * L, L)
            rows = rows_ref[cc, d]
            z, log_w, lw_max, b_last = (rows[k * H:(k + 1) * H, :] for k in range(4))
            cols = cols_ref[cc]
            m_prev = m_ref[d * H:(d + 1) * H, :]
            m_new = jnp.maximum(b_last + m_prev, lw_max)
            w = jnp.exp2(log_w - m_new)
            decay = jnp.exp2((b_last + m_prev) - m_new)
            m_ref[d * H:(d + 1) * H, :] = m_new
            q_all = q_ref[pl.ds(r0, L), :]
            kt_all = kt_ref[cc]
            for hd in range(H):
                stages.append(dict(
                    d=d, sd=d * H + hd, out_ref=out_ref, r0=r0, hd=hd, cols=cols,
                    q=q_all[:, hd * A_DK:(hd + 1) * A_DK], kt=kt_all[hd * A_DK:(hd + 1) * A_DK, :],
                    v_ext=jnp.concatenate([v_ref[pl.ds(r0, L), hd * A_DV:(hd + 1) * A_DV], ones_blk], axis=1),
                    z=z[hd:hd + 1, :], m_prev=m_prev[hd:hd + 1, :], w=w[hd:hd + 1, :],
                    decay=decay[hd:hd + 1, 0:1]))
        for st in stages:
            st["c_prev"] = c_ref[st["sd"]]
            st["qk"] = _dot(st["q"], st["kt"])
        for st in stages:
            lane = 2 * st["d"] * H + st["hd"]
            cm_b = jnp.take_along_axis(st["cols"], jnp.full((L, L), lane, jnp.int32), axis=1,
                                       mode="promise_in_bounds")
            b_b = jnp.take_along_axis(st["cols"], jnp.full((L, L), lane + H, jnp.int32), axis=1,
                                      mode="promise_in_bounds")
            u_b = -jnp.maximum(st["m_prev"], cm_b)
            st["dmat"] = jnp.exp2(jnp.where(keeps[st["d"]], u_b - st["z"], -jnp.inf))
            st["w_inter"] = jnp.exp2(u_b[:, :A_DK] + st["m_prev"][:, :A_DK])
            st["floor"] = jnp.exp2(u_b - b_b)
        for st in stages:
            lhs = jnp.concatenate([(st["qk"] * st["dmat"]).astype(BF16),
                                   (st["w_inter"] * st["q"].astype(F32)).astype(BF16)], axis=1)
            rhs = jnp.concatenate([st["v_ext"], st["c_prev"].astype(BF16)], axis=0)
            st["r"] = _dot(lhs, rhs)
            kw = (st["kt"].astype(F32) * st["w"]).astype(BF16)
            st["upd"] = _dot(kw, st["v_ext"])
        for st in stages:
            r = st["r"]
            floor = st["floor"]
            hd = st["hd"]
            st["out_ref"][pl.ds(st["r0"], L), hd * A_DV:(hd + 1) * A_DV] = \
                (r[:, :L] / jnp.maximum(jnp.abs(r[:, L:]), floor)).astype(BF16)
            c_ref[st["sd"]] = st["decay"] * st["c_prev"] + st["upd"]
        return carry

    lax.fori_loop(0, cps, body, 0)


def _mlstm_scan(q, kt, v, rows, cols, *, tb):
    b, s, _ = q.shape
    cps = tb // A_CHUNK
    nb = s // tb
    ng = 4 * A_HEADS

    def specs(idx):
        return [
            pl.BlockSpec((None, tb, QK_A), lambda i, j: (i, idx(j), 0)),
            pl.BlockSpec((None, cps, QK_A, A_CHUNK), lambda i, j: (i, idx(j), 0, 0)),
            pl.BlockSpec((None, tb, V_A), lambda i, j: (i, idx(j), 0)),
            pl.BlockSpec((None, cps, 2, ng, A_CHUNK), lambda i, j: (i, idx(j), 0, 0, 0)),
            pl.BlockSpec((None, cps, A_CHUNK, A_CHUNK), lambda i, j: (i, idx(j), 0, 0)),
        ]

    fwd = lambda j: j
    bwd = lambda j: nb - 1 - j
    est = 2 * 2 * tb * (QK_A * 2 * 2 + V_A * 2 + V7X_LANES * 4) + 2 * 2 * tb * V_A * 4 \
        + 2 * A_HEADS * A_DK * 2 * A_CHUNK * 4 + (16 << 20)
    return pl.pallas_call(
        functools.partial(_mlstm_scan_kernel, cps=cps),
        grid=(b, nb),
        in_specs=specs(fwd) + specs(bwd),
        out_specs=[
            pl.BlockSpec((None, tb, V_A), lambda i, j: (i, j, 0)),
            pl.BlockSpec((None, tb, V_A), lambda i, j: (i, nb - 1 - j, 0)),
        ],
        out_shape=[jax.ShapeDtypeStruct((b, s, V_A), BF16)] * 2,
        scratch_shapes=[
            pltpu.VMEM((2 * A_HEADS, A_DK, 2 * A_CHUNK), F32),
            pltpu.VMEM((2 * A_HEADS, A_CHUNK), F32),
        ],
        compiler_params=_params(("parallel", "arbitrary"), est),
        name="mlstm_scan",
    )(q, kt, v, rows, cols, q, kt, v, rows, cols)


def _mlstm_out_kernel(x_ref, hf_ref, hb_ref, mod_ref, nw_ref, wo_ref, mnw_ref, wout_ref, o_ref):
    tm = x_ref.shape[0]
    slab = tm // MLSTM_OUT_SLABS
    rows = [pl.ds(r * slab, slab) for r in range(MLSTM_OUT_SLABS)]
    o_gates = []
    for rs in rows:
        h = _norm_mod(x_ref[rs, :], nw_ref[...], mod_ref[3:4, :], mod_ref[4:5, :]).astype(BF16)
        o_gates.append(_dot(h, wo_ref[...]))
    for rs, o_gate in zip(rows, o_gates):
        hs = hf_ref[rs, :].astype(F32) + hb_ref[rs, :].astype(F32)
        parts = []
        for hd in range(A_HEADS):
            a = hs[:, hd * A_DV:(hd + 1) * A_DV]
            ms = jnp.mean(a * a, axis=-1, keepdims=True)
            parts.append(a * lax.rsqrt(ms + EPS))
        y = jnp.concatenate(parts, axis=1) * mnw_ref[...]
        z = (y * _sigmoid(o_gate)).astype(BF16)
        o_ref[rs, :] = x_ref[rs, :] + mod_ref[5:6, :] * _dot(z, wout_ref[...])


def _mlstm_out(x, hf, hb, mod, nw, wo, mnw, wout, *, tm):
    b, s, _ = x.shape
    tile = pl.BlockSpec((None, tm, D_MODEL), lambda i, m: (i, m, 0))
    est = 2 * 4 * tm * D_MODEL * 4 + 2 * D_MODEL * D_MODEL * 2 + 6 * tm * D_MODEL * 4
    return pl.pallas_call(
        _mlstm_out_kernel,
        grid=(b, s // tm),
        in_specs=[
            tile, tile, tile,
            pl.BlockSpec((None, N_MOD, D_MODEL), lambda i, m: (i, 0, 0)),
            pl.BlockSpec((1, D_MODEL), lambda i, m: (0, 0)),
            _resident(wo.shape),
            pl.BlockSpec((1, V_A), lambda i, m: (0, 0)),
            _resident(wout.shape),
        ],
        out_specs=tile,
        out_shape=jax.ShapeDtypeStruct(x.shape, F32),
        compiler_params=_params(("parallel", "parallel"), est),
        name="mlstm_out",
    )(x, hf, hb, mod, nw, wo, mnw, wout)


def _attn_in_kernel(x_ref, mod_ref, nw_ref, wq_ref, wkt_ref, wv_ref, seg_ref,
                    cq_ref, sq_ref, ckt_ref, skt_ref, qw_ref, qwsw_ref, kw_ref, kwsw_ref,
                    q_ref, kt_ref, v_ref, *, q_scale, rot_half):
    h = _norm_mod(x_ref[...], nw_ref[...], mod_ref[3:4, :], mod_ref[4:5, :]).astype(BF16)
    q = _dot(h, wq_ref[...])
    v = _dot(h, wv_ref[...])
    kt = _dot_nt(wkt_ref[...], h)
    ssq = _dot((q * q).astype(BF16), seg_ref[...])

    lane_v = lax.broadcasted_iota(jnp.int32, v.shape, 1)
    v_ref[...] = jnp.where(lane_v % V7X_LANES == HEAD_DIM, 1.0, v).astype(BF16)

    cos_k = ckt_ref[...] * kw_ref[...]
    sin_k = skt_ref[...] * kwsw_ref[...]
    outs = []
    for j in range(ATT_KV_HEADS):
        a = kt[j * HEAD_DIM:(j + 1) * HEAD_DIM, :]
        partner = jnp.concatenate(
            [a[blk + off:blk + off + rot_half, :] for blk in range(0, HEAD_DIM, 2 * rot_half)
             for off in (rot_half, 0)], axis=0)
        rk = lax.rsqrt(jnp.mean(a * a, axis=0, keepdims=True) + EPS)
        outs.append(rk * (a * cos_k + partner * sin_k))
    kt_ref[...] = jnp.concatenate(outs, axis=0).astype(BF16)

    rq = lax.rsqrt(ssq * (1.0 / HEAD_DIM) + EPS)
    cos_q = cq_ref[...] * qw_ref[...]
    sin_q = sq_ref[...] * qwsw_ref[...]
    lane = lax.broadcasted_iota(jnp.int32, cos_q.shape, 1)
    first = lane % (2 * rot_half) < rot_half
    for c in range(Q_ATT // V7X_LANES):
        cols = slice(c * V7X_LANES, (c + 1) * V7X_LANES)
        blk = q[:, cols]
        partner = jnp.where(first, pltpu.roll(blk, V7X_LANES - rot_half, axis=1),
                            pltpu.roll(blk, rot_half, axis=1))
        q_ref[:, cols] = ((rq[:, cols] * q_scale) * (blk * cos_q + partner * sin_q)).astype(BF16)


def _attn_in(x, mod, nw, w, tabs, *, tm, tk, q_scale, rot_half):
    b, s, _ = x.shape
    per = tk // tm
    const2 = lambda i, m: (0, 0)
    est = 2 * tm * D_MODEL * 4 + (2 * D_MODEL * Q_ATT + 3 * D_MODEL * KV_ATT) * 2 \
        + 2 * tm * (Q_ATT + 2 * KV_ATT) * 2 + 8 * tm * Q_ATT * 4 + 8 * tm * V7X_LANES * 4
    return pl.pallas_call(
        functools.partial(_attn_in_kernel, q_scale=q_scale, rot_half=rot_half),
        grid=(b, s // tm),
        in_specs=[
            pl.BlockSpec((None, tm, D_MODEL), lambda i, m: (i, m, 0)),
            pl.BlockSpec((None, N_MOD, D_MODEL), lambda i, m: (i, 0, 0)),
            pl.BlockSpec((1, D_MODEL), const2),
            _resident(w["wq"].shape), _resident(w["wkt"].shape),
            _resident(w["wv"].shape), _resident(w["seg"].shape),
            pl.BlockSpec((tm, V7X_LANES), lambda i, m: (m, 0)),
            pl.BlockSpec((tm, V7X_LANES), lambda i, m: (m, 0)),
            pl.BlockSpec((HEAD_DIM, tm), lambda i, m: (0, m)),
            pl.BlockSpec((HEAD_DIM, tm), lambda i, m: (0, m)),
            pl.BlockSpec((1, V7X_LANES), const2), pl.BlockSpec((1, V7X_LANES), const2),
            pl.BlockSpec((HEAD_DIM, 1), const2), pl.BlockSpec((HEAD_DIM, 1), const2),
        ],
        out_specs=[
            pl.BlockSpec((None, tm, Q_ATT), lambda i, m: (i, m, 0)),
            pl.BlockSpec((None, None, KV_ATT, tm), lambda i, m: (i, m // per, 0, m % per)),
            pl.BlockSpec((None, tm, VX_ATT), lambda i, m: (i, m, 0)),
        ],
        out_shape=[
            jax.ShapeDtypeStruct((b, s, Q_ATT), BF16),
            jax.ShapeDtypeStruct((b, s // tk, KV_ATT, tk), BF16),
            jax.ShapeDtypeStruct((b, s, VX_ATT), BF16),
        ],
        compiler_params=_params(("parallel", "parallel"), est),
        name="attn_in",
    )(x, mod, nw, w["wq"], w["wkt"], w["wv"], w["seg"],
      tabs["cos_q"], tabs["sin_q"], tabs["cos_kt"], tabs["sin_kt"],
      w["qw"], w["qwsw"], w["kw"], w["kwsw"])


def _swa_kernel(bound_ref, sink_ref, q_ref, ktl_ref, ktm_ref, ktr_ref, vl_ref, vm_ref, vr_ref, o_ref,
                *, seq, tq, bounded):
    nsub = tq // BLOCK
    j = pl.program_id(1)
    kt_win = jnp.concatenate([ktl_ref[...], ktm_ref[...], ktr_ref[...]], axis=1)
    v_win = jnp.concatenate([vl_ref[...], vm_ref[...], vr_ref[...]], axis=0)
    qi = lax.broadcasted_iota(jnp.int32, (BLOCK, 3 * BLOCK), 0)
    kj = lax.broadcasted_iota(jnp.int32, (BLOCK, 3 * BLOCK), 1) - BLOCK
    band_bias = jnp.where(jnp.abs(qi - kj) <= WINDOW, 0.0, -jnp.inf)
    biases = []
    for i in range(nsub):
        bias = band_bias
        if i == 0:
            bias = jnp.where(kj + j * tq >= 0, bias, -jnp.inf)
        if i == nsub - 1:
            bias = jnp.where(kj + (j * tq + i * BLOCK) < seq, bias, -jnp.inf)
        biases.append(bias)
    q_tiles = [q_ref[i * BLOCK:(i + 1) * BLOCK, :] for i in range(nsub)]

    per_blk = ATT_HEADS // SWA_STAGE_HEADS
    n_stage = nsub * per_blk

    def heads_of(n):
        i, part = divmod(n, per_blk)
        return i, range(part * SWA_STAGE_HEADS, (part + 1) * SWA_STAGE_HEADS)

    def scores(n):
        i, heads = heads_of(n)
        return [_dot(q_tiles[i][:, h * HEAD_DIM:(h + 1) * HEAD_DIM],
                     kt_win[(h // ATT_GROUP) * HEAD_DIM:(h // ATT_GROUP + 1) * HEAD_DIM, i * BLOCK:(i + 3) * BLOCK])
                + biases[i] for h in heads]

    pending = {n: scores(n) for n in range(min(SWA_AHEAD, n_stage))}
    outs = []
    for n in range(n_stage):
        if n + SWA_AHEAD < n_stage:
            pending[n + SWA_AHEAD] = scores(n + SWA_AHEAD)
        s_heads = pending.pop(n)
        i, heads = heads_of(n)
        sinks = [sink_ref[h] * LOG2E for h in heads]
        if bounded:
            shifts = [jnp.maximum(bound_ref[0], sk) for sk in sinks]
            p = jnp.concatenate([jnp.exp2(s - m) for s, m in zip(s_heads, shifts)], axis=0).astype(BF16)
            sink_w = jnp.concatenate([jnp.full((BLOCK, 1), jnp.exp2(sk - m), F32) for sk, m in zip(sinks, shifts)],
                                     axis=0)
        else:
            s = jnp.concatenate(s_heads, axis=0)
            sink = jnp.concatenate([jnp.full((BLOCK, 1), sk, F32) for sk in sinks], axis=0)
            m = jnp.maximum(jnp.max(s, axis=1, keepdims=True), sink)
            p = jnp.exp2(s - m).astype(BF16)
            sink_w = jnp.exp2(sink - m)
        r = jnp.concatenate(
            [_dot(p[a * BLOCK:(a + 1) * BLOCK, :],
                  v_win[i * BLOCK:(i + 3) * BLOCK, (h // ATT_GROUP) * V7X_LANES:(h // ATT_GROUP + 1) * V7X_LANES])
             for a, h in enumerate(heads)], axis=0)
        o = r[:, :HEAD_DIM] / (r[:, HEAD_DIM:HEAD_DIM + 1] + sink_w)
        outs += [o[a * BLOCK:(a + 1) * BLOCK, :] for a in range(SWA_STAGE_HEADS)]
        if len(outs) == ATT_HEADS:
            o_ref[i * BLOCK:(i + 1) * BLOCK, :] = jnp.concatenate(outs, axis=1).astype(BF16)
            outs = []


def _swa(bound, q, kt, v, sink, *, tq, bounded):
    b, s, _ = q.shape
    nsub = tq // BLOCK
    nblk = s // BLOCK
    nq = s // tq
    left = lambda j: jnp.maximum(j * nsub - 1, 0)
    right = lambda j: jnp.minimum((j + 1) * nsub, nblk - 1)
    est = 2 * (tq * Q_ATT * 2 * 2 + (tq + 2 * BLOCK) * (KV_ATT + VX_ATT) * 2) + (12 << 20)
    return pl.pallas_call(
        functools.partial(_swa_kernel, seq=s, tq=tq, bounded=bounded),
        grid=(b, nq),
        in_specs=[
            pl.BlockSpec(memory_space=pltpu.SMEM),
            pl.BlockSpec(memory_space=pltpu.SMEM),
            pl.BlockSpec((None, tq, Q_ATT), lambda i, j: (i, j, 0)),
            pl.BlockSpec((None, None, KV_ATT, BLOCK), lambda i, j: (i, 0, 0, left(j))),
            pl.BlockSpec((None, None, KV_ATT, tq), lambda i, j: (i, 0, 0, j)),
            pl.BlockSpec((None, None, KV_ATT, BLOCK), lambda i, j: (i, 0, 0, right(j))),
            pl.BlockSpec((None, BLOCK, VX_ATT), lambda i, j: (i, left(j), 0)),
            pl.BlockSpec((None, tq, VX_ATT), lambda i, j: (i, j, 0)),
            pl.BlockSpec((None, BLOCK, VX_ATT), lambda i, j: (i, right(j), 0)),
        ],
        out_specs=pl.BlockSpec((None, tq, Q_ATT), lambda i, j: (i, j, 0)),
        out_shape=jax.ShapeDtypeStruct((b, s, Q_ATT), BF16),
        compiler_params=_params(("parallel", "parallel"), est),
        name="swa_bounded" if bounded else "swa",
    )(bound, sink, q, kt, kt, kt, v, v, v)


def _axial_kernel(bound_ref, q_ref, kt_ref, v_ref, o_ref, m_ref, acc_ref, s_ref, *, tk, nk, bounded):
    q_rows = q_ref[...]
    qs = [q_rows[:, h * HEAD_DIM:(h + 1) * HEAD_DIM] for h in range(ATT_HEADS)]
    m_ref[...] = jnp.full(m_ref.shape, -jnp.inf, F32)
    acc_ref[...] = jnp.zeros_like(acc_ref)

    def scores(h, kt_c):
        g = h // ATT_GROUP
        return _dot(qs[h], kt_c[g * HEAD_DIM:(g + 1) * HEAD_DIM, :])

    kt_0 = kt_ref[0]
    for h in range(AXIAL_AHEAD):
        s_ref[h] = scores(h, kt_0)

    def body(c, carry):
        kt_c = kt_ref[c]
        kt_n = kt_ref[jnp.minimum(c + 1, nk - 1)]
        v_c = v_ref[pl.ds(pl.multiple_of(c * tk, tk), tk), :]
        pending = {h: s_ref[h] for h in range(AXIAL_AHEAD)}
        for h in range(ATT_HEADS):
            ahead = h + AXIAL_AHEAD
            if ahead < ATT_HEADS:
                pending[ahead] = scores(ahead, kt_c)
            else:
                s_ref[ahead - ATT_HEADS] = scores(ahead - ATT_HEADS, kt_n)
            g = h // ATT_GROUP
            s = pending.pop(h)
            v_ext = v_c[:, g * V7X_LANES:(g + 1) * V7X_LANES]
            if bounded:
                acc_ref[h] += _dot(jnp.exp2(s - bound_ref[0]).astype(BF16), v_ext)
            else:
                m_prev = m_ref[h]
                m_new = jnp.maximum(m_prev, jnp.max(s, axis=1, keepdims=True))
                p = jnp.exp2(s - m_new).astype(BF16)
                acc_ref[h] = jnp.exp2(m_prev - m_new) * acc_ref[h] + _dot(p, v_ext)
                m_ref[h] = m_new
        return carry

    lax.fori_loop(0, nk, body, 0, unroll=min(nk, AXIAL_UNROLL))
    outs = []
    for h in range(ATT_HEADS):
        acc = acc_ref[h]
        outs.append(acc[:, :HEAD_DIM] / acc[:, HEAD_DIM:HEAD_DIM + 1])
    o_ref[...] = jnp.concatenate(outs, axis=1).astype(BF16)


def _axial(bound, q, kt, v, *, tq, tk, bounded):
    b, s, _ = q.shape
    nk = s // tk
    rows = ATT_GROUP * tq
    est = 2 * tq * Q_ATT * 2 * 2 + s * (KV_ATT + VX_ATT) * 2 \
        + ATT_KV_HEADS * (2 * rows * V7X_LANES * 4 + rows * tk * 6) + (4 << 20)
    return pl.pallas_call(
        functools.partial(_axial_kernel, tk=tk, nk=nk, bounded=bounded),
        grid=(b, s // tq),
        in_specs=[
            pl.BlockSpec(memory_space=pltpu.SMEM),
            pl.BlockSpec((None, tq, Q_ATT), lambda i, j: (i, j, 0)),
            pl.BlockSpec((None, nk, KV_ATT, tk), lambda i, j: (i, 0, 0, 0), pipeline_mode=pl.Buffered(1)),
            pl.BlockSpec((None, s, VX_ATT), lambda i, j: (i, 0, 0), pipeline_mode=pl.Buffered(1)),
        ],
        out_specs=pl.BlockSpec((None, tq, Q_ATT), lambda i, j: (i, j, 0)),
        out_shape=jax.ShapeDtypeStruct((b, s, Q_ATT), BF16),
        scratch_shapes=[
            pltpu.VMEM((ATT_HEADS, tq, 1), F32), pltpu.VMEM((ATT_HEADS, tq, V7X_LANES), F32),
            pltpu.VMEM((AXIAL_AHEAD, tq, tk), F32),
        ],
        compiler_params=_params(("parallel", "parallel"), est),
        name="axial_bounded" if bounded else "axial",
    )(bound, q, kt, v)


def _rope_tables(pos, dim):
    inv = ROPE_THETA ** (-jnp.arange(0, dim, 2, dtype=F32) / dim)
    ang = pos.astype(F32)[:, None] * inv[None, :]
    ang = jnp.concatenate([ang, ang], axis=-1)
    return jnp.cos(ang), jnp.sin(ang)


def _rot_half_perm(widths):
    perm, sign, base = [], [], 0
    for w in widths:
        half = w // 2
        perm += [base + half + i for i in range(half)] + [base + i for i in range(half)]
        sign += [-1.0] * half + [1.0] * half
        base += w
    return jnp.array(perm, jnp.int32), jnp.array(sign, F32)


def _attn_tables(cos, sin):
    reps = V7X_LANES // HEAD_DIM
    return {
        "cos_q": jnp.tile(cos, (1, reps)), "sin_q": jnp.tile(sin, (1, reps)),
        "cos_kt": cos.T, "sin_kt": sin.T,
    }


def _attn_weights(w_in, q_norm, k_norm, widths):
    perm, sign = _rot_half_perm(widths)
    wq = w_in[:, :Q_ATT]
    wk = w_in[:, Q_ATT:Q_ATT + KV_ATT]
    wv = w_in[:, Q_ATT + KV_ATT:]
    seg_id = jnp.arange(Q_ATT) // HEAD_DIM
    reps = V7X_LANES // HEAD_DIM
    return {
        "wq": wq.astype(BF16), "wkt": wk.T.astype(BF16),
        "wv": jnp.pad(wv.reshape(D_MODEL, ATT_KV_HEADS, HEAD_DIM),
                      ((0, 0), (0, 0), (0, V7X_LANES - HEAD_DIM))).reshape(D_MODEL, VX_ATT).astype(BF16),
        "seg": (seg_id[:, None] == seg_id[None, :]).astype(BF16),
        "qw": jnp.tile(q_norm, reps)[None, :], "qwsw": jnp.tile(q_norm[perm] * sign, reps)[None, :],
        "kw": k_norm[:, None], "kwsw": (k_norm[perm] * sign)[:, None],
        "bound": (BOUND_SLACK * LOG2E * HEAD_DIM ** 0.5
                  * jnp.max(jnp.abs(q_norm)) * jnp.max(jnp.abs(k_norm))).reshape(1).astype(F32),
    }


def _mlstm_weights(w_in, b_gate, norm_w, w_out):
    wq = w_in[:, :QK_A]
    wk = w_in[:, QK_A:2 * QK_A]
    wv = w_in[:, 2 * QK_A:2 * QK_A + V_A]
    wo = w_in[:, 2 * QK_A + V_A:2 * QK_A + 2 * V_A]
    wg = w_in[:, 2 * QK_A + 2 * V_A:]
    return {
        "wq": wq.astype(BF16), "wkt": wk.T.astype(BF16), "wv": wv.astype(BF16), "wo": wo.astype(BF16),
        "wgt": wg.T.astype(BF16), "bgt": b_gate[:, None],
        "mnw": norm_w[None, :], "wout": w_out.astype(BF16),
    }


def _tile(s, want):
    t = min(s, want)
    assert s % t == 0
    return t


def _bounded_or_exact(attend, bound, *operands):
    return lax.cond(bound[0] <= SCORE_BOUND_MAX,
                    functools.partial(attend, bounded=True), functools.partial(attend, bounded=False),
                    bound, *operands)


def _trunk(x, mod_all, prep):
    b, s, _ = x.shape
    tm_ffn = _tile(s, 1024)
    tm = _tile(s, 512)
    for i in range(DEPTH):
        mod = mod_all[i].reshape(b, N_MOD, D_MODEL)
        lw = prep["layers"][i]
        ffn = functools.partial(_ffn, w13=prep["ffn_w13"], w2=prep["ffn_w2"], layer=i, tm=tm_ffn)
        x = ffn(x, mod, lw["nw"][0:1], sub=0, row0=0)
        kind = i % N_MIXERS
        mw = lw["mixer"]
        if kind == 0:
            q, kt, v, rows, cols = _mlstm_in(x, mod, lw["nw"][1:2], mw["wq"], mw["wkt"], mw["wv"],
                                             mw["wgt"], mw["bgt"], tm=tm_ffn)
            hf, hb = _mlstm_scan(q, kt, v, rows, cols, tb=_tile(s, 1024))
            x = _mlstm_out(x, hf, hb, mod, lw["nw"][1:2], mw["wo"], mw["mnw"], mw["wout"], tm=tm_ffn)
        elif kind == 1:
            q, kt, v = _attn_in(x, mod, lw["nw"][1:2], mw, prep["swa_tabs"][s], tm=tm, tk=s,
                                q_scale=HEAD_DIM ** -0.5 * LOG2E, rot_half=HEAD_DIM // 2)
            attn = _bounded_or_exact(functools.partial(_swa, tq=_tile(s, 512)), mw["bound"], q, kt, v, mw["sink"])
        else:
            tk = _tile(s, 1024)
            q, kt, v = _attn_in(x, mod, lw["nw"][1:2], mw, prep["axial_tabs"][s], tm=tm, tk=tk,
                                q_scale=HEAD_DIM ** -0.5 * LOG2E, rot_half=HEAD_DIM // 4)
            attn = _bounded_or_exact(functools.partial(_axial, tq=_tile(s, 128), tk=tk), mw["bound"], q, kt, v)
        if kind == 0:
            x = ffn(x, mod, lw["nw"][2:3], sub=1, row0=6)
        else:
            x = ffn(x, mod, lw["nw"][2:3], sub=1, row0=6, attn=attn, wout=mw["wout"])
    return x


def _prepare(seqs, ffn_w13, ffn_w2, norm_w,
             mlstm_w_in, mlstm_b_gate, mlstm_norm_w, mlstm_w_out,
             swa_w_in, swa_q_norm, swa_k_norm, swa_sink, swa_w_out,
             axial_w_in, axial_q_norm, axial_k_norm, axial_w_out):
    layers = []
    for i in range(DEPTH):
        kind, j = i % N_MIXERS, i // N_MIXERS
        if kind == 0:
            mixer = _mlstm_weights(mlstm_w_in[j], mlstm_b_gate[j], mlstm_norm_w[j], mlstm_w_out[j])
        elif kind == 1:
            mixer = _attn_weights(swa_w_in[j], swa_q_norm[j], swa_k_norm[j], (HEAD_DIM,))
            mixer["sink"] = swa_sink[j]
            mixer["wout"] = swa_w_out[j].astype(BF16)
        else:
            mixer = _attn_weights(axial_w_in[j], axial_q_norm[j], axial_k_norm[j], (HEAD_DIM // 2, HEAD_DIM // 2))
            mixer["wout"] = axial_w_out[j].astype(BF16)
        layers.append({"nw": norm_w[i], "mixer": mixer})
    swa_tabs, axial_tabs = {}, {}
    for s in seqs:
        swa_tabs[s] = _attn_tables(*_rope_tables(jnp.arange(s), HEAD_DIM))
        rows = s // GRID_W
        row_ids = jnp.repeat(jnp.arange(rows), GRID_W)
        col_ids = jnp.tile(jnp.arange(GRID_W), rows)
        rc, rs = _rope_tables(row_ids, HEAD_DIM // 2)
        cc, cs = _rope_tables(col_ids, HEAD_DIM // 2)
        axial_tabs[s] = _attn_tables(jnp.concatenate([rc, cc], axis=-1), jnp.concatenate([rs, cs], axis=-1))
    return {"layers": layers, "swa_tabs": swa_tabs, "axial_tabs": axial_tabs,
            "ffn_w13": ffn_w13.astype(BF16), "ffn_w2": ffn_w2.astype(BF16)}


def kernel(x_prompt, x_sample, c_prompt, c_sample, ffn_w13, ffn_w2, ada_w, ada_b, norm_w, mlstm_w_in, mlstm_b_gate, mlstm_norm_w, mlstm_w_out, swa_w_in, swa_q_norm, swa_k_norm, swa_sink, swa_w_out, axial_w_in, axial_q_norm, axial_k_norm, axial_w_out):
    prep = _prepare({x_prompt.shape[1], x_sample.shape[1]}, ffn_w13, ffn_w2, norm_w,
                    mlstm_w_in, mlstm_b_gate, mlstm_norm_w, mlstm_w_out,
                    swa_w_in, swa_q_norm, swa_k_norm, swa_sink, swa_w_out,
                    axial_w_in, axial_q_norm, axial_k_norm, axial_w_out)
    nb_prompt = c_prompt.shape[0]
    mod_all = _ada_mod(jnp.concatenate([c_prompt, c_sample], axis=0), ada_w, ada_b)
    y_prompt = _trunk(x_prompt, mod_all[:, :nb_prompt], prep)
    y_sample = _trunk(x_sample, mod_all[:, nb_prompt:], prep)
    return (y_prompt, y_sample)
```

```python
import functools

import jax
import jax.numpy as jnp
from jax import lax
from jax.experimental import pallas as pl
from jax.experimental.pallas import tpu as pltpu

F32 = jnp.float32
BF16 = jnp.bfloat16

D_MODEL = 1024
DEPTH = 4
N_MIXERS = 3
D_FF = 2816
EPS = 1e-6
N_MOD = 9
A_HEADS = 8
A_DK = 64
A_DV = 128
A_CHUNK = 128
A_GATE_CAP = 15.0
ATT_HEADS = 16
ATT_KV_HEADS = 4
ATT_GROUP = ATT_HEADS // ATT_KV_HEADS
HEAD_DIM = 64
WINDOW = 128
BLOCK = 128
ROPE_THETA = 10000.0
GRID_W = 64

V7X_VMEM_BYTES = 64 * 1024 * 1024
V7X_LANES = 128
V7X_MXU_DIM = 256

FFN_CHUNK = V7X_MXU_DIM
N_FFN_CHUNKS = D_FF // FFN_CHUNK
QK_A = A_HEADS * A_DK
V_A = A_HEADS * A_DV
Q_ATT = ATT_HEADS * HEAD_DIM
KV_ATT = ATT_KV_HEADS * HEAD_DIM
VX_ATT = ATT_KV_HEADS * V7X_LANES
LOG2E = 1.4426950408889634
AXIAL_AHEAD = 3
AXIAL_UNROLL = 4
SWA_AHEAD = 2
BOUND_SLACK = 1.02
SCORE_BOUND_MAX = 48.0
SWA_STAGE_HEADS = 2
MLSTM_OUT_SLABS = 2

NT_DIMS = (((1,), (1,)), ((), ()))


def _vmem_limit(est_bytes):
    return int(min(est_bytes * 5 // 4 + (4 << 20), V7X_VMEM_BYTES - (4 << 20)))


def _params(sem, est_bytes):
    return pltpu.CompilerParams(dimension_semantics=sem, vmem_limit_bytes=_vmem_limit(est_bytes))


def _resident(shape):
    nd = len(shape)
    return pl.BlockSpec(shape, lambda *_: (0,) * nd, pipeline_mode=pl.Buffered(1))


def _dot(a, b):
    return jnp.dot(a, b, preferred_element_type=F32)


def _dot_nt(a, b):
    return lax.dot_general(a, b, NT_DIMS, preferred_element_type=F32)


def _norm_mod(x, nw, shift, scale):
    ms = jnp.mean(x * x, axis=-1, keepdims=True)
    y = x * lax.rsqrt(ms + EPS)
    return (y * nw) * (1.0 + scale) + shift


def _sigmoid(x):
    return 1.0 / (1.0 + jnp.exp(-x))


def _ada_kernel(c_ref, w_ref, b_ref, o_ref):
    c = c_ref[...]
    a = c * _sigmoid(c)
    o_ref[...] = _dot(a, w_ref[...]) + b_ref[...]


def _ada_mod(c_all, ada_w, ada_b):
    bc = c_all.shape[0]
    n_out = N_MOD * D_MODEL
    tn = D_MODEL
    return pl.pallas_call(
        _ada_kernel,
        grid=(DEPTH, n_out // tn),
        in_specs=[
            pl.BlockSpec((bc, D_MODEL), lambda i, n: (0, 0)),
            pl.BlockSpec((None, D_MODEL, tn), lambda i, n: (i, 0, n)),
            pl.BlockSpec((None, 1, tn), lambda i, n: (i, 0, n)),
        ],
        out_specs=pl.BlockSpec((None, bc, tn), lambda i, n: (i, 0, n)),
        out_shape=jax.ShapeDtypeStruct((DEPTH, bc, n_out), F32),
        compiler_params=_params(("parallel", "parallel"), 4 * (2 * D_MODEL * tn + 4 * bc * tn)),
        name="ada_mod",
    )(c_all, ada_w, ada_b.reshape(DEPTH, 1, n_out))


def _ffn_kernel(*refs, row0, mixer_out):
    if mixer_out:
        x_ref, attn_ref, wout_ref, mod_ref, nw_ref, w13_ref, w2_ref, o_ref, h_ref, a_ref = refs
        o_ref[...] = x_ref[...] + mod_ref[5:6, :] * _dot(attn_ref[...], wout_ref[...])
        x_ref = o_ref
    else:
        x_ref, mod_ref, nw_ref, w13_ref, w2_ref, o_ref, h_ref, a_ref = refs
    shift = mod_ref[row0:row0 + 1, :]
    scale = mod_ref[row0 + 1:row0 + 2, :]
    gate = mod_ref[row0 + 2:row0 + 3, :]
    h_ref[...] = _norm_mod(x_ref[...], nw_ref[...], shift, scale).astype(BF16)

    for c in range(N_FFN_CHUNKS):
        cols = slice(c * FFN_CHUNK, (c + 1) * FFN_CHUNK)
        h = h_ref[...]
        g = _dot(h, w13_ref[:, cols])
        u = _dot(h, w13_ref[:, D_FF + c * FFN_CHUNK:D_FF + (c + 1) * FFN_CHUNK])
        a_ref[:, cols] = ((g * _sigmoid(g)) * u).astype(BF16)
    o_ref[...] = x_ref[...] + (0.5 * gate) * _dot(a_ref[...], w2_ref[...])


def _ffn(x, mod, nw, w13, w2, *, layer, sub, row0, tm, attn=None, wout=None):
    b, s, _ = x.shape

    def stacked(shape):
        return pl.BlockSpec((None, None) + shape, lambda i, m: (layer, sub, 0, 0), pipeline_mode=pl.Buffered(1))
    tile = pl.BlockSpec((None, tm, D_MODEL), lambda i, m: (i, m, 0))
    mixer_out = attn is not None
    est = 4 * tm * D_MODEL * 4 + 3 * D_MODEL * D_FF * 2 + tm * (D_MODEL + D_FF) * 2 + 4 * tm * FFN_CHUNK * 4 \
        + tm * D_MODEL * 4
    mixer_specs, mixer_args = [], []
    if mixer_out:
        est += 2 * tm * D_MODEL * 2 + D_MODEL * D_MODEL * 2
        mixer_specs, mixer_args = [tile, _resident(wout.shape)], [attn, wout]
    return pl.pallas_call(
        functools.partial(_ffn_kernel, row0=row0, mixer_out=mixer_out),
        grid=(b, s // tm),
        in_specs=[tile] + mixer_specs + [
            pl.BlockSpec((None, N_MOD, D_MODEL), lambda i, m: (i, 0, 0)),
            pl.BlockSpec((1, D_MODEL), lambda i, m: (0, 0)),
            stacked(w13.shape[2:]),
            stacked(w2.shape[2:]),
        ],
        out_specs=tile,
        out_shape=jax.ShapeDtypeStruct(x.shape, F32),
        scratch_shapes=[pltpu.VMEM((tm, D_MODEL), BF16), pltpu.VMEM((tm, D_FF), BF16)],
        compiler_params=_params(("parallel", "parallel"), est),
        name="ffn_mixer_out" if mixer_out else "ffn",
    )(x, *mixer_args, mod, nw, w13, w2)


def _log_sigmoid(x):
    return jnp.minimum(x, 0.0) - jnp.log1p(jnp.exp(-jnp.abs(x)))


def _gate_act(g, is_forget):
    g = A_GATE_CAP * jnp.tanh(g / A_GATE_CAP)
    return jnp.where(is_forget, _log_sigmoid(g), g)


def _mlstm_in_kernel(x_ref, mod_ref, nw_ref, wq_ref, wkt_ref, wv_ref, wgt_ref, bgt_ref,
                     q_ref, kt_ref, v_ref, rows_ref, cols_ref):
    L, H = A_CHUNK, A_HEADS
    cpt = kt_ref.shape[0]
    h = _norm_mod(x_ref[...], nw_ref[...], mod_ref[3:4, :], mod_ref[4:5, :]).astype(BF16)
    g_row = _dot_nt(wgt_ref[...], h) + bgt_ref[...]
    q_ref[...] = (_dot(h, wq_ref[...]) * (A_DK ** -0.5)).astype(BF16)

    row_id = lax.broadcasted_iota(jnp.int32, g_row.shape, 0)
    g_row = _gate_act(g_row, (row_id // H) % 2 == 1) * LOG2E
    ri = lax.broadcasted_iota(jnp.int32, (L, L), 0)
    ci = lax.broadcasted_iota(jnp.int32, (L, L), 1)
    tris = (jnp.where(ri <= ci, 1.0, 0.0).astype(BF16), jnp.where(ri >= ci, 1.0, 0.0).astype(BF16))
    eye = jnp.where(ri == ci, 1.0, 0.0).astype(BF16)
    lane = lax.broadcasted_iota(jnp.int32, (cpt * H, L), 1)

    def chunk_rows(first_row):
        return jnp.concatenate([g_row[first_row:first_row + H, j * L:(j + 1) * L] for j in range(cpt)], axis=0)

    igs = [chunk_rows(2 * d * H) for d in range(2)]
    bs = [_cumsum_rows(chunk_rows((2 * d + 1) * H), tris[d]) for d in range(2)]
    v_ref[...] = _dot(h, wv_ref[...]).astype(BF16)

    col_parts = []
    for d in range(2):
        b, ig = bs[d], igs[d]
        z = b - ig
        cm = _cummax_lanes(-z, lane, reverse=(d == 1))
        b_last = b[:, L - 1:L] if d == 0 else b[:, 0:1]
        log_w = (b_last - b) + ig
        lw_max = jnp.broadcast_to(jnp.max(log_w, axis=1, keepdims=True), b.shape)
        b_last = jnp.broadcast_to(b_last, b.shape)
        for j in range(cpt):
            rows_ref[j, d] = jnp.concatenate([a[j * H:(j + 1) * H, :] for a in (z, log_w, lw_max, b_last)], axis=0)
        col_parts.append((cm, b))
    kt = _dot_nt(wkt_ref[...], h).astype(BF16)
    for j in range(cpt):
        kt_ref[j] = kt[:, j * L:(j + 1) * L]
        packed = jnp.concatenate([a[j * H:(j + 1) * H, :] for d in range(2) for a in col_parts[d]]
                                 + [jnp.zeros((L - 4 * H, L), F32)], axis=0)
        cols_ref[j] = _transpose_rows(eye, packed)


def _mlstm_in(x, mod, nw, wq, wkt, wv, wgt, bgt, *, tm):
    b, s, _ = x.shape
    nch, cpt = s // A_CHUNK, tm // A_CHUNK
    ng = 4 * A_HEADS
    est = 2 * tm * D_MODEL * 4 + 2 * (D_MODEL * (2 * QK_A + V_A + V7X_LANES)) * 2 \
        + 2 * tm * (QK_A * 2 * 2 + V_A * 2 + 2 * V7X_LANES * 4) + tm * (2 * QK_A + V_A) * 4 + (4 << 20)
    return pl.pallas_call(
        _mlstm_in_kernel,
        grid=(b, s // tm),
        in_specs=[
            pl.BlockSpec((None, tm, D_MODEL), lambda i, m: (i, m, 0)),
            pl.BlockSpec((None, N_MOD, D_MODEL), lambda i, m: (i, 0, 0)),
            pl.BlockSpec((1, D_MODEL), lambda i, m: (0, 0)),
            _resident(wq.shape), _resident(wkt.shape), _resident(wv.shape),
            _resident(wgt.shape), _resident(bgt.shape),
        ],
        out_specs=[
            pl.BlockSpec((None, tm, QK_A), lambda i, m: (i, m, 0)),
            pl.BlockSpec((None, cpt, QK_A, A_CHUNK), lambda i, m: (i, m, 0, 0)),
            pl.BlockSpec((None, tm, V_A), lambda i, m: (i, m, 0)),
            pl.BlockSpec((None, cpt, 2, ng, A_CHUNK), lambda i, m: (i, m, 0, 0, 0)),
            pl.BlockSpec((None, cpt, A_CHUNK, A_CHUNK), lambda i, m: (i, m, 0, 0)),
        ],
        out_shape=[
            jax.ShapeDtypeStruct((b, s, QK_A), BF16),
            jax.ShapeDtypeStruct((b, nch, QK_A, A_CHUNK), BF16),
            jax.ShapeDtypeStruct((b, s, V_A), BF16),
            jax.ShapeDtypeStruct((b, nch, 2, ng, A_CHUNK), F32),
            jax.ShapeDtypeStruct((b, nch, A_CHUNK, A_CHUNK), F32),
        ],
        compiler_params=_params(("parallel", "parallel"), est),
        name="mlstm_in",
    )(x, mod, nw, wq, wkt, wv, wgt, bgt)


def _split3(a):
    hi = a.astype(BF16)
    r1 = a - hi.astype(F32)
    mid = r1.astype(BF16)
    lo = (r1 - mid.astype(F32)).astype(BF16)
    return hi, mid, lo


def _cumsum_rows(a, tri):
    hi, mid, lo = _split3(a)
    return _dot(hi, tri) + _dot(mid, tri) + _dot(lo, tri)


def _transpose_rows(eye, a):
    hi, mid, lo = _split3(a)
    return _dot_nt(eye, hi) + _dot_nt(eye, mid) + _dot_nt(eye, lo)


def _cummax_lanes(x, lane, reverse):
    n = x.shape[1]
    k = 1
    while k < n:
        if reverse:
            shifted = jnp.where(lane < n - k, pltpu.roll(x, n - k, axis=1), -jnp.inf)
        else:
            shifted = jnp.where(lane >= k, pltpu.roll(x, k, axis=1), -jnp.inf)
        x = jnp.maximum(x, shifted)
        k *= 2
    return x


def _mlstm_scan_kernel(qf_ref, ktf_ref, vf_ref, rowsf_ref, colsf_ref, qb_ref, ktb_ref, vb_ref, rowsb_ref, colsb_ref,
                       hf_ref, hb_ref, c_ref, m_ref, *, cps):
    L = A_CHUNK
    H = A_HEADS

    @pl.when(pl.program_id(1) == 0)
    def _():
        c_ref[...] = jnp.zeros_like(c_ref)
        m_ref[...] = jnp.full(m_ref.shape, -jnp.inf, F32)

    ri = lax.broadcasted_iota(jnp.int32, (L, L), 0)
    ci = lax.broadcasted_iota(jnp.int32, (L, L), 1)
    keeps = (ri >= ci, ri <= ci)
    ones_blk = jnp.ones((L, L), BF16)
    refs = ((qf_ref, ktf_ref, vf_ref, rowsf_ref, colsf_ref, hf_ref),
            (qb_ref, ktb_ref, vb_ref, rowsb_ref, colsb_ref, hb_ref))

    def body(c, carry):
        stages = []
        for d in range(2):
            q_ref, kt_ref, v_ref, rows_ref, cols_ref, out_ref = refs[d]
            cc = c if d == 0 else cps - 1 - c
            r0 = pl.multiple_of(cc * L, L)
            rows = rows_ref[cc, d]
            z, log_w, lw_max, b_last = (rows[k * H:(k + 1) * H, :] for k in range(4))
            cols = cols_ref[cc]
            m_prev = m_ref[d * H:(d + 1) * H, :]
            m_new = jnp.maximum(b_last + m_prev, lw_max)
            w = jnp.exp2(log_w - m_new)
            decay = jnp.exp2((b_last + m_prev) - m_new)
            m_ref[d * H:(d + 1) * H, :] = m_new
            q_all = q_ref[pl.ds(r0, L), :]
            kt_all = kt_ref[cc]
            for hd in range(H):
                stages.append(dict(
                    d=d, sd=d * H + hd, out_ref=out_ref, r0=r0, hd=hd, cols=cols,
                    q=q_all[:, hd * A_DK:(hd + 1) * A_DK], kt=kt_all[hd * A_DK:(hd + 1) * A_DK, :],
                    v_ext=jnp.concatenate([v_ref[pl.ds(r0, L), hd * A_DV:(hd + 1) * A_DV], ones_blk], axis=1),
                    z=z[hd:hd + 1, :], m_prev=m_prev[hd:hd + 1, :], w=w[hd:hd + 1, :],
                    decay=decay[hd:hd + 1, 0:1]))
        for st in stages:
            st["c_prev"] = c_ref[st["sd"]]
            st["qk"] = _dot(st["q"], st["kt"])
        for st in stages:
            lane = 2 * st["d"] * H + st["hd"]
            cm_b = jnp.take_along_axis(st["cols"], jnp.full((L, L), lane, jnp.int32), axis=1,
                                       mode="promise_in_bounds")
            b_b = jnp.take_along_axis(st["cols"], jnp.full((L, L), lane + H, jnp.int32), axis=1,
                                      mode="promise_in_bounds")
            u_b = -jnp.maximum(st["m_prev"], cm_b)
            st["dmat"] = jnp.exp2(jnp.where(keeps[st["d"]], u_b - st["z"], -jnp.inf))
            st["w_inter"] = jnp.exp2(u_b[:, :A_DK] + st["m_prev"][:, :A_DK])
            st["floor"] = jnp.exp2(u_b - b_b)
        for st in stages:
            lhs = jnp.concatenate([(st["qk"] * st["dmat"]).astype(BF16),
                                   (st["w_inter"] * st["q"].astype(F32)).astype(BF16)], axis=1)
            rhs = jnp.concatenate([st["v_ext"], st["c_prev"].astype(BF16)], axis=0)
            st["r"] = _dot(lhs, rhs)
            kw = (st["kt"].astype(F32) * st["w"]).astype(BF16)
            st["upd"] = _dot(kw, st["v_ext"])
        for st in stages:
            r = st["r"]
            floor = st["floor"]
            hd = st["hd"]
            st["out_ref"][pl.ds(st["r0"], L), hd * A_DV:(hd + 1) * A_DV] = \
                (r[:, :L] / jnp.maximum(jnp.abs(r[:, L:]), floor)).astype(BF16)
            c_ref[st["sd"]] = st["decay"] * st["c_prev"] + st["upd"]
        return carry

    lax.fori_loop(0, cps, body, 0)


def _mlstm_scan(q, kt, v, rows, cols, *, tb):
    b, s, _ = q.shape
    cps = tb // A_CHUNK
    nb = s // tb
    ng = 4 * A_HEADS

    def specs(idx):
        return [
            pl.BlockSpec((None, tb, QK_A), lambda i, j: (i, idx(j), 0)),
            pl.BlockSpec((None, cps, QK_A, A_CHUNK), lambda i, j: (i, idx(j), 0, 0)),
            pl.BlockSpec((None, tb, V_A), lambda i, j: (i, idx(j), 0)),
            pl.BlockSpec((None, cps, 2, ng, A_CHUNK), lambda i, j: (i, idx(j), 0, 0, 0)),
            pl.BlockSpec((None, cps, A_CHUNK, A_CHUNK), lambda i, j: (i, idx(j), 0, 0)),
        ]

    fwd = lambda j: j
    bwd = lambda j: nb - 1 - j
    est = 2 * 2 * tb * (QK_A * 2 * 2 + V_A * 2 + V7X_LANES * 4) + 2 * 2 * tb * V_A * 4 \
        + 2 * A_HEADS * A_DK * 2 * A_CHUNK * 4 + (16 << 20)
    return pl.pallas_call(
        functools.partial(_mlstm_scan_kernel, cps=cps),
        grid=(b, nb),
        in_specs=specs(fwd) + specs(bwd),
        out_specs=[
            pl.BlockSpec((None, tb, V_A), lambda i, j: (i, j, 0)),
            pl.BlockSpec((None, tb, V_A), lambda i, j: (i, nb - 1 - j, 0)),
        ],
        out_shape=[jax.ShapeDtypeStruct((b, s, V_A), BF16)] * 2,
        scratch_shapes=[
            pltpu.VMEM((2 * A_HEADS, A_DK, 2 * A_CHUNK), F32),
            pltpu.VMEM((2 * A_HEADS, A_CHUNK), F32),
        ],
        compiler_params=_params(("parallel", "arbitrary"), est),
        name="mlstm_scan",
    )(q, kt, v, rows, cols, q, kt, v, rows, cols)


def _mlstm_out_kernel(x_ref, hf_ref, hb_ref, mod_ref, nw_ref, wo_ref, mnw_ref, wout_ref, o_ref):
    tm = x_ref.shape[0]
    slab = tm // MLSTM_OUT_SLABS
    rows = [pl.ds(r * slab, slab) for r in range(MLSTM_OUT_SLABS)]
    o_gates = []
    for rs in rows:
        h = _norm_mod(x_ref[rs, :], nw_ref[...], mod_ref[3:4, :], mod_ref[4:5, :]).astype(BF16)
        o_gates.append(_dot(h, wo_ref[...]))
    for rs, o_gate in zip(rows, o_gates):
        hs = hf_ref[rs, :].astype(F32) + hb_ref[rs, :].astype(F32)
        parts = []
        for hd in range(A_HEADS):
            a = hs[:, hd * A_DV:(hd + 1) * A_DV]
            ms = jnp.mean(a * a, axis=-1, keepdims=True)
            parts.append(a * lax.rsqrt(ms + EPS))
        y = jnp.concatenate(parts, axis=1) * mnw_ref[...]
        z = (y * _sigmoid(o_gate)).astype(BF16)
        o_ref[rs, :] = x_ref[rs, :] + mod_ref[5:6, :] * _dot(z, wout_ref[...])


def _mlstm_out(x, hf, hb, mod, nw, wo, mnw, wout, *, tm):
    b, s, _ = x.shape
    tile = pl.BlockSpec((None, tm, D_MODEL), lambda i, m: (i, m, 0))
    est = 2 * 4 * tm * D_MODEL * 4 + 2 * D_MODEL * D_MODEL * 2 + 6 * tm * D_MODEL * 4
    return pl.pallas_call(
        _mlstm_out_kernel,
        grid=(b, s // tm),
        in_specs=[
            tile, tile, tile,
            pl.BlockSpec((None, N_MOD, D_MODEL), lambda i, m: (i, 0, 0)),
            pl.BlockSpec((1, D_MODEL), lambda i, m: (0, 0)),
            _resident(wo.shape),
            pl.BlockSpec((1, V_A), lambda i, m: (0, 0)),
            _resident(wout.shape),
        ],
        out_specs=tile,
        out_shape=jax.ShapeDtypeStruct(x.shape, F32),
        compiler_params=_params(("parallel", "parallel"), est),
        name="mlstm_out",
    )(x, hf, hb, mod, nw, wo, mnw, wout)


def _attn_in_kernel(x_ref, mod_ref, nw_ref, wq_ref, wkt_ref, wv_ref, seg_ref,
                    cq_ref, sq_ref, ckt_ref, skt_ref, qw_ref, qwsw_ref, kw_ref, kwsw_ref,
                    q_ref, kt_ref, v_ref, *, q_scale, rot_half):
    h = _norm_mod(x_ref[...], nw_ref[...], mod_ref[3:4, :], mod_ref[4:5, :]).astype(BF16)
    q = _dot(h, wq_ref[...])
    v = _dot(h, wv_ref[...])
    kt = _dot_nt(wkt_ref[...], h)
    ssq = _dot((q * q).astype(BF16), seg_ref[...])

    lane_v = lax.broadcasted_iota(jnp.int32, v.shape, 1)
    v_ref[...] = jnp.where(lane_v % V7X_LANES == HEAD_DIM, 1.0, v).astype(BF16)

    cos_k = ckt_ref[...] * kw_ref[...]
    sin_k = skt_ref[...] * kwsw_ref[...]
    outs = []
    for j in range(ATT_KV_HEADS):
        a = kt[j * HEAD_DIM:(j + 1) * HEAD_DIM, :]
        partner = jnp.concatenate(
            [a[blk + off:blk + off + rot_half, :] for blk in range(0, HEAD_DIM, 2 * rot_half)
             for off in (rot_half, 0)], axis=0)
        rk = lax.rsqrt(jnp.mean(a * a, axis=0, keepdims=True) + EPS)
        outs.append(rk * (a * cos_k + partner * sin_k))
    kt_ref[...] = jnp.concatenate(outs, axis=0).astype(BF16)

    rq = lax.rsqrt(ssq * (1.0 / HEAD_DIM) + EPS)
    cos_q = cq_ref[...] * qw_ref[...]
    sin_q = sq_ref[...] * qwsw_ref[...]
    lane = lax.broadcasted_iota(jnp.int32, cos_q.shape, 1)
    first = lane % (2 * rot_half) < rot_half
    for c in range(Q_ATT // V7X_LANES):
        cols = slice(c * V7X_LANES, (c + 1) * V7X_LANES)
        blk = q[:, cols]
        partner = jnp.where(first, pltpu.roll(blk, V7X_LANES - rot_half, axis=1),
                            pltpu.roll(blk, rot_half, axis=1))
        q_ref[:, cols] = ((rq[:, cols] * q_scale) * (blk * cos_q + partner * sin_q)).astype(BF16)


def _attn_in(x, mod, nw, w, tabs, *, tm, tk, q_scale, rot_half):
    b, s, _ = x.shape
    per = tk // tm
    const2 = lambda i, m: (0, 0)
    est = 2 * tm * D_MODEL * 4 + (2 * D_MODEL * Q_ATT + 3 * D_MODEL * KV_ATT) * 2 \
        + 2 * tm * (Q_ATT + 2 * KV_ATT) * 2 + 8 * tm * Q_ATT * 4 + 8 * tm * V7X_LANES * 4
    return pl.pallas_call(
        functools.partial(_attn_in_kernel, q_scale=q_scale, rot_half=rot_half),
        grid=(b, s // tm),
        in_specs=[
            pl.BlockSpec((None, tm, D_MODEL), lambda i, m: (i, m, 0)),
            pl.BlockSpec((None, N_MOD, D_MODEL), lambda i, m: (i, 0, 0)),
            pl.BlockSpec((1, D_MODEL), const2),
            _resident(w["wq"].shape), _resident(w["wkt"].shape),
            _resident(w["wv"].shape), _resident(w["seg"].shape),
            pl.BlockSpec((tm, V7X_LANES), lambda i, m: (m, 0)),
            pl.BlockSpec((tm, V7X_LANES), lambda i, m: (m, 0)),
            pl.BlockSpec((HEAD_DIM, tm), lambda i, m: (0, m)),
            pl.BlockSpec((HEAD_DIM, tm), lambda i, m: (0, m)),
            pl.BlockSpec((1, V7X_LANES), const2), pl.BlockSpec((1, V7X_LANES), const2),
            pl.BlockSpec((HEAD_DIM, 1), const2), pl.BlockSpec((HEAD_DIM, 1), const2),
        ],
        out_specs=[
            pl.BlockSpec((None, tm, Q_ATT), lambda i, m: (i, m, 0)),
            pl.BlockSpec((None, None, KV_ATT, tm), lambda i, m: (i, m // per, 0, m % per)),
            pl.BlockSpec((None, tm, VX_ATT), lambda i, m: (i, m, 0)),
        ],
        out_shape=[
            jax.ShapeDtypeStruct((b, s, Q_ATT), BF16),
            jax.ShapeDtypeStruct((b, s // tk, KV_ATT, tk), BF16),
            jax.ShapeDtypeStruct((b, s, VX_ATT), BF16),
        ],
        compiler_params=_params(("parallel", "parallel"), est),
        name="attn_in",
    )(x, mod, nw, w["wq"], w["wkt"], w["wv"], w["seg"],
      tabs["cos_q"], tabs["sin_q"], tabs["cos_kt"], tabs["sin_kt"],
      w["qw"], w["qwsw"], w["kw"], w["kwsw"])


def _swa_kernel(bound_ref, sink_ref, q_ref, ktl_ref, ktm_ref, ktr_ref, vl_ref, vm_ref, vr_ref, o_ref,
                *, seq, tq, bounded):
    nsub = tq // BLOCK
    j = pl.program_id(1)
    kt_win = jnp.concatenate([ktl_ref[...], ktm_ref[...], ktr_ref[...]], axis=1)
    v_win = jnp.concatenate([vl_ref[...], vm_ref[...], vr_ref[...]], axis=0)
    qi = lax.broadcasted_iota(jnp.int32, (BLOCK, 3 * BLOCK), 0)
    kj = lax.broadcasted_iota(jnp.int32, (BLOCK, 3 * BLOCK), 1) - BLOCK
    band_bias = jnp.where(jnp.abs(qi - kj) <= WINDOW, 0.0, -jnp.inf)
    biases = []
    for i in range(nsub):
        bias = band_bias
        if i == 0:
            bias = jnp.where(kj + j * tq >= 0, bias, -jnp.inf)
        if i == nsub - 1:
            bias = jnp.where(kj + (j * tq + i * BLOCK) < seq, bias, -jnp.inf)
        biases.append(bias)
    q_tiles = [q_ref[i * BLOCK:(i + 1) * BLOCK, :] for i in range(nsub)]

    per_blk = ATT_HEADS // SWA_STAGE_HEADS
    n_stage = nsub * per_blk

    def heads_of(n):
        i, part = divmod(n, per_blk)
        return i, range(part * SWA_STAGE_HEADS, (part + 1) * SWA_STAGE_HEADS)

    def scores(n):
        i, heads = heads_of(n)
        return [_dot(q_tiles[i][:, h * HEAD_DIM:(h + 1) * HEAD_DIM],
                     kt_win[(h // ATT_GROUP) * HEAD_DIM:(h // ATT_GROUP + 1) * HEAD_DIM, i * BLOCK:(i + 3) * BLOCK])
                + biases[i] for h in heads]

    pending = {n: scores(n) for n in range(min(SWA_AHEAD, n_stage))}
    outs = []
    for n in range(n_stage):
        if n + SWA_AHEAD < n_stage:
            pending[n + SWA_AHEAD] = scores(n + SWA_AHEAD)
        s_heads = pending.pop(n)
        i, heads = heads_of(n)
        sinks = [sink_ref[h] * LOG2E for h in heads]
        if bounded:
            shifts = [jnp.maximum(bound_ref[0], sk) for sk in sinks]
            p = jnp.concatenate([jnp.exp2(s - m) for s, m in zip(s_heads, shifts)], axis=0).astype(BF16)
            sink_w = jnp.concatenate([jnp.full((BLOCK, 1), jnp.exp2(sk - m), F32) for sk, m in zip(sinks, shifts)],
                                     axis=0)
        else:
            s = jnp.concatenate(s_heads, axis=0)
            sink = jnp.concatenate([jnp.full((BLOCK, 1), sk, F32) for sk in sinks], axis=0)
            m = jnp.maximum(jnp.max(s, axis=1, keepdims=True), sink)
            p = jnp.exp2(s - m).astype(BF16)
            sink_w = jnp.exp2(sink - m)
        r = jnp.concatenate(
            [_dot(p[a * BLOCK:(a + 1) * BLOCK, :],
                  v_win[i * BLOCK:(i + 3) * BLOCK, (h // ATT_GROUP) * V7X_LANES:(h // ATT_GROUP + 1) * V7X_LANES])
             for a, h in enumerate(heads)], axis=0)
        o = r[:, :HEAD_DIM] / (r[:, HEAD_DIM:HEAD_DIM + 1] + sink_w)
        outs += [o[a * BLOCK:(a + 1) * BLOCK, :] for a in range(SWA_STAGE_HEADS)]
        if len(outs) == ATT_HEADS:
            o_ref[i * BLOCK:(i + 1) * BLOCK, :] = jnp.concatenate(outs, axis=1).astype(BF16)
            outs = []


def _swa(bound, q, kt, v, sink, *, tq, bounded):
    b, s, _ = q.shape
    nsub = tq // BLOCK
    nblk = s // BLOCK
    nq = s // tq
    left = lambda j: jnp.maximum(j * nsub - 1, 0)
    right = lambda j: jnp.minimum((j + 1) * nsub, nblk - 1)
    est = 2 * (tq * Q_ATT * 2 * 2 + (tq + 2 * BLOCK) * (KV_ATT + VX_ATT) * 2) + (12 << 20)
    return pl.pallas_call(
        functools.partial(_swa_kernel, seq=s, tq=tq, bounded=bounded),
        grid=(b, nq),
        in_specs=[
            pl.BlockSpec(memory_space=pltpu.SMEM),
            pl.BlockSpec(memory_space=pltpu.SMEM),
            pl.BlockSpec((None, tq, Q_ATT), lambda i, j: (i, j, 0)),
            pl.BlockSpec((None, None, KV_ATT, BLOCK), lambda i, j: (i, 0, 0, left(j))),
            pl.BlockSpec((None, None, KV_ATT, tq), lambda i, j: (i, 0, 0, j)),
            pl.BlockSpec((None, None, KV_ATT, BLOCK), lambda i, j: (i, 0, 0, right(j))),
            pl.BlockSpec((None, BLOCK, VX_ATT), lambda i, j: (i, left(j), 0)),
            pl.BlockSpec((None, tq, VX_ATT), lambda i, j: (i, j, 0)),
            pl.BlockSpec((None, BLOCK, VX_ATT), lambda i, j: (i, right(j), 0)),
        ],
        out_specs=pl.BlockSpec((None, tq, Q_ATT), lambda i, j: (i, j, 0)),
        out_shape=jax.ShapeDtypeStruct((b, s, Q_ATT), BF16),
        compiler_params=_params(("parallel", "parallel"), est),
        name="swa_bounded" if bounded else "swa",
    )(bound, sink, q, kt, kt, kt, v, v, v)


def _axial_kernel(bound_ref, q_ref, kt_ref, v_ref, o_ref, m_ref, acc_ref, s_ref, *, tk, nk, bounded):
    q_rows = q_ref[...]
    qs = [q_rows[:, h * HEAD_DIM:(h + 1) * HEAD_DIM] for h in range(ATT_HEADS)]
    m_ref[...] = jnp.full(m_ref.shape, -jnp.inf, F32)
    acc_ref[...] = jnp.zeros_like(acc_ref)

    def scores(h, kt_c):
        g = h // ATT_GROUP
        return _dot(qs[h], kt_c[g * HEAD_DIM:(g + 1) * HEAD_DIM, :])

    kt_0 = kt_ref[0]
    for h in range(AXIAL_AHEAD):
        s_ref[h] = scores(h, kt_0)

    def body(c, carry):
        kt_c = kt_ref[c]
        kt_n = kt_ref[jnp.minimum(c + 1, nk - 1)]
        v_c = v_ref[pl.ds(pl.multiple_of(c * tk, tk), tk), :]
        pending = {h: s_ref[h] for h in range(AXIAL_AHEAD)}
        for h in range(ATT_HEADS):
            ahead = h + AXIAL_AHEAD
            if ahead < ATT_HEADS:
                pending[ahead] = scores(ahead, kt_c)
            else:
                s_ref[ahead - ATT_HEADS] = scores(ahead - ATT_HEADS, kt_n)
            g = h // ATT_GROUP
            s = pending.pop(h)
            v_ext = v_c[:, g * V7X_LANES:(g + 1) * V7X_LANES]
            if bounded:
                acc_ref[h] += _dot(jnp.exp2(s - bound_ref[0]).astype(BF16), v_ext)
            else:
                m_prev = m_ref[h]
                m_new = jnp.maximum(m_prev, jnp.max(s, axis=1, keepdims=True))
                p = jnp.exp2(s - m_new).astype(BF16)
                acc_ref[h] = jnp.exp2(m_prev - m_new) * acc_ref[h] + _dot(p, v_ext)
                m_ref[h] = m_new
        return carry

    lax.fori_loop(0, nk, body, 0, unroll=min(nk, AXIAL_UNROLL))
    outs = []
    for h in range(ATT_HEADS):
        acc = acc_ref[h]
        outs.append(acc[:, :HEAD_DIM] / acc[:, HEAD_DIM:HEAD_DIM + 1])
    o_ref[...] = jnp.concatenate(outs, axis=1).astype(BF16)


def _axial(bound, q, kt, v, *, tq, tk, bounded):
    b, s, _ = q.shape
    nk = s // tk
    rows = ATT_GROUP * tq
    est = 2 * tq * Q_ATT * 2 * 2 + s * (KV_ATT + VX_ATT) * 2 \
        + ATT_KV_HEADS * (2 * rows * V7X_LANES * 4 + rows * tk * 6) + (4 << 20)
    return pl.pallas_call(
        functools.partial(_axial_kernel, tk=tk, nk=nk, bounded=bounded),
        grid=(b, s // tq),
        in_specs=[
            pl.BlockSpec(memory_space=pltpu.SMEM),
            pl.BlockSpec((None, tq, Q_ATT), lambda i, j: (i, j, 0)),
            pl.BlockSpec((None, nk, KV_ATT, tk), lambda i, j: (i, 0, 0, 0), pipeline_mode=pl.Buffered(1)),
            pl.BlockSpec((None, s, VX_ATT), lambda i, j: (i, 0, 0), pipeline_mode=pl.Buffered(1)),
        ],
        out_specs=pl.BlockSpec((None, tq, Q_ATT), lambda i, j: (i, j, 0)),
        out_shape=jax.ShapeDtypeStruct((b, s, Q_ATT), BF16),
        scratch_shapes=[
            pltpu.VMEM((ATT_HEADS, tq, 1), F32), pltpu.VMEM((ATT_HEADS, tq, V7X_LANES), F32),
            pltpu.VMEM((AXIAL_AHEAD, tq, tk), F32),
        ],
        compiler_params=_params(("parallel", "parallel"), est),
        name="axial_bounded" if bounded else "axial",
    )(bound, q, kt, v)


def _rope_tables(pos, dim):
    inv = ROPE_THETA ** (-jnp.arange(0, dim, 2, dtype=F32) / dim)
    ang = pos.astype(F32)[:, None] * inv[None, :]
    ang = jnp.concatenate([ang, ang], axis=-1)
    return jnp.cos(ang), jnp.sin(ang)


def _rot_half_perm(widths):
    perm, sign, base = [], [], 0
    for w in widths:
        half = w // 2
        perm += [base + half + i for i in range(half)] + [base + i for i in range(half)]
        sign += [-1.0] * half + [1.0] * half
        base += w
    return jnp.array(perm, jnp.int32), jnp.array(sign, F32)


def _attn_tables(cos, sin):
    reps = V7X_LANES // HEAD_DIM
    return {
        "cos_q": jnp.tile(cos, (1, reps)), "sin_q": jnp.tile(sin, (1, reps)),
        "cos_kt": cos.T, "sin_kt": sin.T,
    }


def _attn_weights(w_in, q_norm, k_norm, widths):
    perm, sign = _rot_half_perm(widths)
    wq = w_in[:, :Q_ATT]
    wk = w_in[:, Q_ATT:Q_ATT + KV_ATT]
    wv = w_in[:, Q_ATT + KV_ATT:]
    seg_id = jnp.arange(Q_ATT) // HEAD_DIM
    reps = V7X_LANES // HEAD_DIM
    return {
        "wq": wq.astype(BF16), "wkt": wk.T.astype(BF16),
        "wv": jnp.pad(wv.reshape(D_MODEL, ATT_KV_HEADS, HEAD_DIM),
                      ((0, 0), (0, 0), (0, V7X_LANES - HEAD_DIM))).reshape(D_MODEL, VX_ATT).astype(BF16),
        "seg": (seg_id[:, None] == seg_id[None, :]).astype(BF16),
        "qw": jnp.tile(q_norm, reps)[None, :], "qwsw": jnp.tile(q_norm[perm] * sign, reps)[None, :],
        "kw": k_norm[:, None], "kwsw": (k_norm[perm] * sign)[:, None],
        "bound": (BOUND_SLACK * LOG2E * HEAD_DIM ** 0.5
                  * jnp.max(jnp.abs(q_norm)) * jnp.max(jnp.abs(k_norm))).reshape(1).astype(F32),
    }


def _mlstm_weights(w_in, b_gate, norm_w, w_out):
    wq = w_in[:, :QK_A]
    wk = w_in[:, QK_A:2 * QK_A]
    wv = w_in[:, 2 * QK_A:2 * QK_A + V_A]
    wo = w_in[:, 2 * QK_A + V_A:2 * QK_A + 2 * V_A]
    wg = w_in[:, 2 * QK_A + 2 * V_A:]
    return {
        "wq": wq.astype(BF16), "wkt": wk.T.astype(BF16), "wv": wv.astype(BF16), "wo": wo.astype(BF16),
        "wgt": wg.T.astype(BF16), "bgt": b_gate[:, None],
        "mnw": norm_w[None, :], "wout": w_out.astype(BF16),
    }


def _tile(s, want):
    t = min(s, want)
    assert s % t == 0
    return t


def _bounded_or_exact(attend, bound, *operands):
    return lax.cond(bound[0] <= SCORE_BOUND_MAX,
                    functools.partial(attend, bounded=True), functools.partial(attend, bounded=False),
                    bound, *operands)


def _trunk(x, mod_all, prep):
    b, s, _ = x.shape
    tm_ffn = _tile(s, 1024)
    tm = _tile(s, 512)
    for i in range(DEPTH):
        mod = mod_all[i].reshape(b, N_MOD, D_MODEL)
        lw = prep["layers"][i]
        ffn = functools.partial(_ffn, w13=prep["ffn_w13"], w2=prep["ffn_w2"], layer=i, tm=tm_ffn)
        x = ffn(x, mod, lw["nw"][0:1], sub=0, row0=0)
        kind = i % N_MIXERS
        mw = lw["mixer"]
        if kind == 0:
            q, kt, v, rows, cols = _mlstm_in(x, mod, lw["nw"][1:2], mw["wq"], mw["wkt"], mw["wv"],
                                             mw["wgt"], mw["bgt"], tm=tm_ffn)
            hf, hb = _mlstm_scan(q, kt, v, rows, cols, tb=_tile(s, 1024))
            x = _mlstm_out(x, hf, hb, mod, lw["nw"][1:2], mw["wo"], mw["mnw"], mw["wout"], tm=tm_ffn)
        elif kind == 1:
            q, kt, v = _attn_in(x, mod, lw["nw"][1:2], mw, prep["swa_tabs"][s], tm=tm, tk=s,
                                q_scale=HEAD_DIM ** -0.5 * LOG2E, rot_half=HEAD_DIM // 2)
            attn = _bounded_or_exact(functools.partial(_swa, tq=_tile(s, 512)), mw["bound"], q, kt, v, mw["sink"])
        else:
            tk = _tile(s, 1024)
            q, kt, v = _attn_in(x, mod, lw["nw"][1:2], mw, prep["axial_tabs"][s], tm=tm, tk=tk,
                                q_scale=HEAD_DIM ** -0.5 * LOG2E, rot_half=HEAD_DIM // 4)
            attn = _bounded_or_exact(functools.partial(_axial, tq=_tile(s, 256), tk=tk), mw["bound"], q, kt, v)
        if kind == 0:
            x = ffn(x, mod, lw["nw"][2:3], sub=1, row0=6)
        else:
            x = ffn(x, mod, lw["nw"][2:3], sub=1, row0=6, attn=attn, wout=mw["wout"])
    return x


def _prepare(seqs, ffn_w13, ffn_w2, norm_w,
             mlstm_w_in, mlstm_b_gate, mlstm_norm_w, mlstm_w_out,
             swa_w_in, swa_q_norm, swa_k_norm, swa_sink, swa_w_out,
             axial_w_in, axial_q_norm, axial_k_norm, axial_w_out):
    layers = []
    for i in range(DEPTH):
        kind, j = i % N_MIXERS, i // N_MIXERS
        if kind == 0:
            mixer = _mlstm_weights(mlstm_w_in[j], mlstm_b_gate[j], mlstm_norm_w[j], mlstm_w_out[j])
        elif kind == 1:
            mixer = _attn_weights(swa_w_in[j], swa_q_norm[j], swa_k_norm[j], (HEAD_DIM,))
            mixer["sink"] = swa_sink[j]
            mixer["wout"] = swa_w_out[j].astype(BF16)
        else:
            mixer = _attn_weights(axial_w_in[j], axial_q_norm[j], axial_k_norm[j], (HEAD_DIM // 2, HEAD_DIM // 2))
            mixer["wout"] = axial_w_out[j].astype(BF16)
        layers.append({"nw": norm_w[i], "mixer": mixer})
    swa_tabs, axial_tabs = {}, {}
    for s in seqs:
        swa_tabs[s] = _attn_tables(*_rope_tables(jnp.arange(s), HEAD_DIM))
        rows = s // GRID_W
        row_ids = jnp.repeat(jnp.arange(rows), GRID_W)
        col_ids = jnp.tile(jnp.arange(GRID_W), rows)
        rc, rs = _rope_tables(row_ids, HEAD_DIM // 2)
        cc, cs = _rope_tables(col_ids, HEAD_DIM // 2)
        axial_tabs[s] = _attn_tables(jnp.concatenate([rc, cc], axis=-1), jnp.concatenate([rs, cs], axis=-1))
    return {"layers": layers, "swa_tabs": swa_tabs, "axial_tabs": axial_tabs,
            "ffn_w13": ffn_w13.astype(BF16), "ffn_w2": ffn_w2.astype(BF16)}


def kernel(x_prompt, x_sample, c_prompt, c_sample, ffn_w13, ffn_w2, ada_w, ada_b, norm_w, mlstm_w_in, mlstm_b_gate, mlstm_norm_w, mlstm_w_out, swa_w_in, swa_q_norm, swa_k_norm, swa_sink, swa_w_out, axial_w_in, axial_q_norm, axial_k_norm, axial_w_out):
    prep = _prepare({x_prompt.shape[1], x_sample.shape[1]}, ffn_w13, ffn_w2, norm_w,
                    mlstm_w_in, mlstm_b_gate, mlstm_norm_w, mlstm_w_out,
                    swa_w_in, swa_q_norm, swa_k_norm, swa_sink, swa_w_out,
                    axial_w_in, axial_q_norm, axial_k_norm, axial_w_out)
    nb_prompt = c_prompt.shape[0]
    mod_all = _ada_mod(jnp.concatenate([c_prompt, c_sample], axis=0), ada_w, ada_b)
    y_prompt = _trunk(x_prompt, mod_all[:, :nb_prompt], prep)
    y_sample = _trunk(x_sample, mod_all[:, nb_prompt:], prep)
    return (y_prompt, y_sample)
```
